```python
import math
import jax, jax.numpy as jnp
from jax import lax
import numpy as np

D_MODEL = 1024
BATCH = 2
SEQ = 8192
DEPTH = 4
DEC_BATCH = 128
DEC_SEQ = 8
PAST_LEN = 2048
PAGE_SIZE = 128

EPS = 1e-5
A_HEADS = 4
A_DH = 64
A_QK = A_HEADS * 2 * A_DH
A_V = A_HEADS * 2 * A_DH
Q_BLOCK = 128
POOL_WINDOWS = (2, 4, 8, 16)
POOL_GROUPS = len(POOL_WINDOWS)
POOL_GDIM = D_MODEL // 8
POOL_DIM = POOL_GROUPS * POOL_GDIM
POOL_MAX = max(POOL_WINDOWS)
MIX_IN = 2 * A_QK + A_V + POOL_DIM
MIX_OUT = A_V + POOL_DIM
CONV_W = 3
N_MEM = 256
MEM_HEADS = 4
MEM_DH = D_MODEL // MEM_HEADS
PEER_HEADS = 8
PEER_NKEYS = 128
PEER_EXPERTS = PEER_NKEYS * PEER_NKEYS
PEER_DKEY = 256
PEER_HALF = PEER_DKEY // 2
PEER_TOPK = 16
PEER_BLOCK = 128
N_AB_LAYERS = (DEPTH + 1) // 2
N_C_LAYERS = DEPTH // 2

kernel_name = "hybrid_diffattn_pool_shortconv_peer_decoder_step"


def rmsnorm(x, g):
    x32 = x.astype(jnp.float32)
    y = x32 * lax.rsqrt(jnp.mean(x32 * x32, axis=-1, keepdims=True) + EPS)
    return y.astype(x.dtype) * g


def gather_pages(pool, layer, page_table):
    g = pool[layer, page_table]
    return g.reshape(g.shape[0], g.shape[1] * g.shape[2], *g.shape[3:])


def diff_attn_core(q, k, v, q_pos, k_pos, lam):
    s = jnp.einsum("bqhmd,bkhmd->bhmqk", q, k).astype(jnp.float32) * (A_DH ** -0.5)
    mask = k_pos[None, :] <= q_pos[:, None]
    s = jnp.where(mask, s, jnp.finfo(jnp.float32).min)
    p = jax.nn.softmax(s, axis=-1)
    a = p[:, :, 0] - lam * p[:, :, 1]
    return jnp.einsum("bhqk,bkhe->bqhe", a.astype(v.dtype), v)


def pool_mix(p, past, pos0, pool_w, pool_scale):
    B, T, _ = p.shape
    full = jnp.concatenate([past, p], axis=1).astype(jnp.float32)
    cs = jnp.concatenate([jnp.zeros((B, 1, POOL_DIM), jnp.float32), jnp.cumsum(full, axis=1)], axis=1)
    pos = pos0 + jnp.arange(T)
    means = []
    for g, w in enumerate(POOL_WINDOWS):
        sl = slice(g * POOL_GDIM, (g + 1) * POOL_GDIM)
        s = cs[:, POOL_MAX:POOL_MAX + T, sl] - cs[:, POOL_MAX - w:POOL_MAX - w + T, sl]
        cnt = jnp.minimum(w, pos + 1).astype(jnp.float32)[None, :, None]
        means.append(s / cnt)
    mean = jnp.stack(means, axis=2)
    d = (mean - p.reshape(B, T, POOL_GROUPS, POOL_GDIM).astype(jnp.float32)).astype(p.dtype)
    y = jnp.einsum("btgc,gcd->btgd", d, pool_w).reshape(B, T, POOL_DIM)
    return y * pool_scale


def mixer_ab(h, pos0, past_k, past_v, pool_past, w_in, w_out, lq1, lk1, lq2, lk2, subln_g,
             pool_w, pool_scale, lam_init):
    B, T, _ = h.shape
    proj = h @ w_in
    q, k, v, p = jnp.split(proj, [A_QK, 2 * A_QK, 2 * A_QK + A_V], axis=-1)
    q = q.reshape(B, T, A_HEADS, 2, A_DH)
    k = k.reshape(B, T, A_HEADS, 2, A_DH)
    v = v.reshape(B, T, A_HEADS, 2 * A_DH)
    lam = (jnp.exp(jnp.sum(lq1.astype(jnp.float32) * lk1.astype(jnp.float32)))
           - jnp.exp(jnp.sum(lq2.astype(jnp.float32) * lk2.astype(jnp.float32))) + lam_init)
    q_pos = pos0 + jnp.arange(T)
    if past_k is None:
        nblk = T // Q_BLOCK
        qb = jnp.moveaxis(q.reshape(B, nblk, Q_BLOCK, A_HEADS, 2, A_DH), 1, 0)
        pb = q_pos.reshape(nblk, Q_BLOCK)
        o = lax.map(lambda qp: diff_attn_core(qp[0], k, v, qp[1], q_pos, lam), (qb, pb))
        o = jnp.moveaxis(o, 0, 1).reshape(B, T, A_HEADS, 2 * A_DH)
    else:
        P = past_k.shape[1]
        k_all = jnp.concatenate([past_k.reshape(B, P, A_HEADS, 2, A_DH), k], axis=1)
        v_all = jnp.concatenate([past_v, v], axis=1)
        o = diff_attn_core(q, k_all, v_all, q_pos, jnp.arange(P + T), lam)
    o = (rmsnorm(o, subln_g) * (1.0 - lam_init)).reshape(B, T, A_V)
    pooled = pool_mix(p, pool_past, pos0, pool_w, pool_scale)
    out = jnp.concatenate([o, pooled], axis=-1) @ w_out
    new_pool = jnp.concatenate([pool_past, p], axis=1)[:, -(POOL_MAX - 1):]
    return out, k.reshape(B, T, A_HEADS, 2 * A_DH), v, new_pool


def short_conv(h, w_in, conv_w, w_out, conv_past):
    T = h.shape[1]
    b, c, xx = jnp.split(h @ w_in, 3, axis=-1)
    u = c * xx
    full = jnp.concatenate([conv_past, u], axis=1)
    y = conv_w[0] * full[:, 0:T]
    for j in range(1, CONV_W):
        y = y + conv_w[j] * full[:, j:j + T]
    return (b * y) @ w_out, full[:, -(CONV_W - 1):]


def mem_attn(h, mk, mv, wq, wo):
    B, T, _ = h.shape
    q = (h @ wq).reshape(B, T, MEM_HEADS, MEM_DH)
    s = jnp.einsum("bthd,bmhd->bhtm", q, mk).astype(jnp.float32) * (MEM_DH ** -0.5)
    p = jax.nn.softmax(s, axis=-1)
    o = jnp.einsum("bhtm,bmhd->bthd", p.astype(mv.dtype), mv).reshape(B, T, D_MODEL)
    return o @ wo


def peer(h, wq, subkeys, u_tab, v_tab):
    B, T, D = h.shape
    n = B * T
    nblk = -(-n // PEER_BLOCK)
    xf = jnp.pad(h.reshape(n, D), ((0, nblk * PEER_BLOCK - n), (0, 0)))

    def blk(xb):
        q = (xb @ wq).reshape(PEER_BLOCK, PEER_HEADS, 2, PEER_HALF)
        s = jnp.einsum("nhjc,hjkc->nhjk", q, subkeys).astype(jnp.float32)
        sv, si = lax.top_k(s, PEER_TOPK)
        cand = sv[:, :, 0, :, None] + sv[:, :, 1, None, :]
        cid = si[:, :, 0, :, None] * PEER_NKEYS + si[:, :, 1, None, :]
        top_s, top_j = lax.top_k(cand.reshape(PEER_BLOCK, PEER_HEADS, PEER_TOPK * PEER_TOPK), PEER_TOPK)
        eid = jnp.take_along_axis(cid.reshape(PEER_BLOCK, PEER_HEADS, PEER_TOPK * PEER_TOPK), top_j, axis=-1)
        g = jax.nn.softmax(top_s, axis=-1)
        a = jax.nn.gelu(jnp.einsum("nhkd,nd->nhk", u_tab[eid], xb), approximate=False)
        return jnp.einsum("nhk,nhkd->nd", (g * a).astype(xb.dtype), v_tab[eid])

    out = lax.map(blk, xf.reshape(nblk, PEER_BLOCK, D))
    return out.reshape(-1, D)[:n].reshape(B, T, D)


def trunk(x, pos0, mem_k, mem_v, pool_past, conv_past, cache_k, cache_v, page_table, W):
    new_k, new_v, new_pool, new_conv = [], [], [], []
    for l in range(DEPTH):
        i = l // 2
        h = rmsnorm(x, W["norm_mix"][l])
        if l % 2 == 0:
            if page_table is None:
                pk, pv = None, None
            else:
                pk = gather_pages(cache_k, i, page_table)
                pv = gather_pages(cache_v, i, page_table)
            lam_init = 0.8 - 0.6 * math.exp(-0.3 * l)
            out, k_l, v_l, p_l = mixer_ab(h, pos0, pk, pv, pool_past[i], W["w_in_ab"][i], W["w_out_ab"][i],
                                          W["lambda_q1"][i], W["lambda_k1"][i], W["lambda_q2"][i],
                                          W["lambda_k2"][i], W["subln_g"][i], W["pool_w"][i],
                                          W["pool_scale"][i], lam_init)
            new_k.append(k_l)
            new_v.append(v_l)
            new_pool.append(p_l)
        else:
            out, c_l = short_conv(h, W["w_in_c"][i], W["conv_w"][i], W["w_out_c"][i], conv_past[i])
            new_conv.append(c_l)
        x = x + out
        x = x + mem_attn(rmsnorm(x, W["norm_mem"][l]), mem_k[l], mem_v[l], W["w_mq"][l], W["w_mo"][l])
        x = x + peer(rmsnorm(x, W["norm_ffn"][l]), W["peer_wq"][l], W["peer_subkeys"][l],
                     W["peer_u"][l], W["peer_v"][l])
    y = rmsnorm(x, W["norm_final"])
    return y, jnp.stack(new_k), jnp.stack(new_v), jnp.stack(new_pool), jnp.stack(new_conv)


def setup_inputs(seed: int = 0) -> dict:
    key = jax.random.key(seed)
    ks = iter(jax.random.split(key, 64))

    def nrm(shape, scale):
        return jax.random.normal(next(ks), shape, jnp.float32) * scale

    n_pages = PAST_LEN // PAGE_SIZE
    n_used = DEC_BATCH * n_pages
    n_pool = n_used + n_used // 4
    page_table = jax.random.permutation(next(ks), n_pool)[:n_used].reshape(DEC_BATCH, n_pages).astype(jnp.int32)
    return {
        "x_prompt": nrm((BATCH, SEQ, D_MODEL), 1.0),
        "x_sample": nrm((DEC_BATCH, DEC_SEQ, D_MODEL), 1.0),
        "cache_attn_k": nrm((N_AB_LAYERS, n_pool, PAGE_SIZE, A_HEADS, 2 * A_DH), 1.0),
        "cache_attn_v": nrm((N_AB_LAYERS, n_pool, PAGE_SIZE, A_HEADS, 2 * A_DH), 1.0),
        "state_pool": nrm((N_AB_LAYERS, DEC_BATCH, POOL_MAX - 1, POOL_DIM), 1.0),
        "state_conv": nrm((N_C_LAYERS, DEC_BATCH, CONV_W - 1, D_MODEL), 1.0),
        "cache_mem_k": nrm((DEPTH, DEC_BATCH, N_MEM, MEM_HEADS, MEM_DH), 1.0),
        "cache_mem_v": nrm((DEPTH, DEC_BATCH, N_MEM, MEM_HEADS, MEM_DH), 1.0),
        "page_table": page_table,
        "mem_prompt": nrm((BATCH, N_MEM, D_MODEL), 1.0),
        "norm_mix": 1.0 + nrm((DEPTH, D_MODEL), 0.1),
        "norm_mem": 1.0 + nrm((DEPTH, D_MODEL), 0.1),
        "norm_ffn": 1.0 + nrm((DEPTH, D_MODEL), 0.1),
        "norm_final": 1.0 + nrm((D_MODEL,), 0.1),
        "w_in_ab": nrm((N_AB_LAYERS, D_MODEL, MIX_IN), D_MODEL ** -0.5),
        "w_out_ab": nrm((N_AB_LAYERS, MIX_OUT, D_MODEL), MIX_OUT ** -0.5),
        "lambda_q1": nrm((N_AB_LAYERS, A_DH), 0.1),
        "lambda_k1": nrm((N_AB_LAYERS, A_DH), 0.1),
        "lambda_q2": nrm((N_AB_LAYERS, A_DH), 0.1),
        "lambda_k2": nrm((N_AB_LAYERS, A_DH), 0.1),
        "subln_g": 1.0 + nrm((N_AB_LAYERS, 2 * A_DH), 0.1),
        "pool_w": nrm((N_AB_LAYERS, POOL_GROUPS, POOL_GDIM, POOL_GDIM), POOL_GDIM ** -0.5),
        "pool_scale": 1.0 + nrm((N_AB_LAYERS, POOL_DIM), 0.1),
        "w_in_c": nrm((N_C_LAYERS, D_MODEL, 3 * D_MODEL), D_MODEL ** -0.5),
        "conv_w": nrm((N_C_LAYERS, CONV_W, D_MODEL), CONV_W ** -0.5),
        "w_out_c": nrm((N_C_LAYERS, D_MODEL, D_MODEL), D_MODEL ** -0.5),
        "w_mq": nrm((DEPTH, D_MODEL, D_MODEL), D_MODEL ** -0.5),
        "w_mk": nrm((DEPTH, D_MODEL, D_MODEL), D_MODEL ** -0.5),
        "w_mv": nrm((DEPTH, D_MODEL, D_MODEL), D_MODEL ** -0.5),
        "w_mo": nrm((DEPTH, D_MODEL, D_MODEL), D_MODEL ** -0.5),
        "peer_wq": nrm((DEPTH, D_MODEL, PEER_HEADS * PEER_DKEY), D_MODEL ** -0.5),
        "peer_subkeys": nrm((DEPTH, PEER_HEADS, 2, PEER_NKEYS, PEER_HALF), PEER_HALF ** -0.5),
        "peer_u": nrm((DEPTH, PEER_EXPERTS, D_MODEL), D_MODEL ** -0.5),
        "peer_v": nrm((DEPTH, PEER_EXPERTS, D_MODEL), (PEER_HEADS * PEER_TOPK) ** -0.5),
    }


def reference(x_prompt, x_sample, cache_attn_k, cache_attn_v, state_pool, state_conv, cache_mem_k,
              cache_mem_v, page_table, mem_prompt, norm_mix, norm_mem, norm_ffn, norm_final, w_in_ab,
              w_out_ab, lambda_q1, lambda_k1, lambda_q2, lambda_k2, subln_g, pool_w, pool_scale, w_in_c,
              conv_w, w_out_c, w_mq, w_mk, w_mv, w_mo, peer_wq, peer_subkeys, peer_u, peer_v):
    W = {"norm_mix": norm_mix, "norm_mem": norm_mem, "norm_ffn": norm_ffn, "norm_final": norm_final,
         "w_in_ab": w_in_ab, "w_out_ab": w_out_ab, "lambda_q1": lambda_q1, "lambda_k1": lambda_k1,
         "lambda_q2": lambda_q2, "lambda_k2": lambda_k2, "subln_g": subln_g, "pool_w": pool_w,
         "pool_scale": pool_scale, "w_in_c": w_in_c, "conv_w": conv_w, "w_out_c": w_out_c,
         "w_mq": w_mq, "w_mo": w_mo, "peer_wq": peer_wq, "peer_subkeys": peer_subkeys,
         "peer_u": peer_u, "peer_v": peer_v}
    B = x_prompt.shape[0]
    mk_p = jnp.einsum("bmd,lde->lbme", mem_prompt, w_mk).reshape(DEPTH, B, N_MEM, MEM_HEADS, MEM_DH)
    mv_p = jnp.einsum("bmd,lde->lbme", mem_prompt, w_mv).reshape(DEPTH, B, N_MEM, MEM_HEADS, MEM_DH)
    pool0 = jnp.zeros((N_AB_LAYERS, B, POOL_MAX - 1, POOL_DIM), x_prompt.dtype)
    conv0 = jnp.zeros((N_C_LAYERS, B, CONV_W - 1, D_MODEL), x_prompt.dtype)
    y_prompt, k_p, v_p, pool_p, conv_p = trunk(x_prompt, 0, mk_p, mv_p, pool0, conv0, None, None, None, W)
    past_len = page_table.shape[1] * cache_attn_k.shape[2]
    y_sample, k_s, v_s, pool_s, conv_s = trunk(x_sample, past_len, cache_mem_k, cache_mem_v, state_pool,
                                               state_conv, cache_attn_k, cache_attn_v, page_table, W)
    return (y_prompt, y_sample, k_p, v_p, pool_p, conv_p, mk_p, mv_p, k_s, v_s, pool_s, conv_s)
```

```python
import functools
import math

import jax
import jax.numpy as jnp
from jax import lax
from jax.experimental import pallas as pl
from jax.experimental.pallas import tpu as pltpu

F32 = jnp.float32
BF16 = jnp.bfloat16

D_MODEL = 1024
DEPTH = 4
EPS = 1e-5
A_HEADS = 4
A_DH = 64
A_HD = 2 * A_DH
A_QK = A_HEADS * A_HD
POOL_WINDOWS = (2, 4, 8, 16)
POOL_GDIM = 128
POOL_DIM = 512
POOL_MAX = 16
CONV_W = 3
N_MEM = 256
MEM_HEADS = 4
MEM_DH = 256
PEER_HEADS = 8
PEER_NKEYS = 128
PEER_EXPERTS = PEER_NKEYS * PEER_NKEYS
PEER_TOPK = 16
PAGE_SIZE = 128

VMEM_LIMIT = 56 * 1024 * 1024
NEG_INF = float("-inf")
MASK_VALUE = float(jnp.finfo(jnp.float32).min)


def _params(sem, vmem=VMEM_LIMIT):
    return pltpu.CompilerParams(dimension_semantics=sem, vmem_limit_bytes=vmem)


def _rms(x, g):
    return x * lax.rsqrt(jnp.mean(x * x, axis=-1, keepdims=True) + EPS) * g


def _gelu(a):
    return 0.5 * a * (1.0 + lax.erf(a * (2.0 ** -0.5)))


def _dot(a, b):
    return jnp.dot(a, b, preferred_element_type=F32)


def _dot_nt(a, b):
    return lax.dot_general(a, b, (((1,), (1,)), ((), ())), preferred_element_type=F32)


def _linear_kernel(*refs, n_in, has_norm, has_res, splits, gate):
    a_refs = refs[:n_in]
    w_refs = refs[n_in:2 * n_in]
    idx = 2 * n_in
    g_ref = refs[idx] if has_norm else None
    idx += int(has_norm)
    r_ref = refs[idx] if has_res else None
    idx += int(has_res)
    out_refs = refs[idx:]
    acc = None
    for a_ref, w_ref in zip(a_refs, w_refs):
        a = a_ref[...]
        if has_norm:
            a = _rms(a, g_ref[...])
        d = _dot(a.astype(BF16), w_ref[...])
        acc = d if acc is None else acc + d
    if has_res:
        acc = acc + r_ref[...]
    if gate:
        d3 = acc.shape[1] // 3
        out_refs[0][...] = acc[:, :d3]
        out_refs[1][...] = acc[:, d3:2 * d3] * acc[:, 2 * d3:]
    else:
        off = 0
        for o_ref, s in zip(out_refs, splits):
            o_ref[...] = acc[:, off:off + s]
            off += s


def fused_linear(a_list, w_list, *, name, gain=None, res=None, splits=None, gate=False, tn=512):
    n = a_list[0].shape[0]
    m = w_list[0].shape[1]
    tn = min(tn, n)
    assert n % tn == 0
    if gate:
        out_cols = [m // 3, m // 3]
    else:
        out_cols = list(splits) if splits is not None else [m]
        assert sum(out_cols) == m
    in_specs, args = [], []
    for a in a_list:
        in_specs.append(pl.BlockSpec((tn, a.shape[1]), lambda i: (i, 0)))
        args.append(a)
    for w in w_list:
        in_specs.append(pl.BlockSpec(w.shape, lambda i: (0, 0)))
        args.append(w)
    if gain is not None:
        in_specs.append(pl.BlockSpec((1, gain.shape[-1]), lambda i: (0, 0)))
        args.append(gain.reshape(1, -1))
    if res is not None:
        in_specs.append(pl.BlockSpec((tn, m), lambda i: (i, 0)))
        args.append(res)
    out_shape = [jax.ShapeDtypeStruct((n, c), F32) for c in out_cols]
    out_specs = [pl.BlockSpec((tn, c), lambda i: (i, 0)) for c in out_cols]
    kern = functools.partial(_linear_kernel, n_in=len(a_list), has_norm=gain is not None,
                             has_res=res is not None, splits=tuple(out_cols), gate=gate)
    outs = pl.pallas_call(
        kern, name=name, grid=(n // tn,), in_specs=in_specs, out_specs=out_specs, out_shape=out_shape,
        compiler_params=_params(("parallel",)))(*args)
    return outs if len(outs) > 1 else outs[0]


def _memproj_kernel(a_ref, wk_ref, wv_ref, ok_ref, ov_ref):
    a = a_ref[...].astype(BF16)
    ok_ref[0] = _dot(a, wk_ref[0])
    ov_ref[0] = _dot(a, wv_ref[0])


def mem_project(mem2d, wk, wv):
    n, d = mem2d.shape
    depth = wk.shape[0]
    spec_w = pl.BlockSpec((1, d, d), lambda l: (l, 0, 0))
    spec_o = pl.BlockSpec((1, n, d), lambda l: (l, 0, 0))
    return pl.pallas_call(
        _memproj_kernel, name="mem_project", grid=(depth,),
        in_specs=[pl.BlockSpec((n, d), lambda l: (0, 0)), spec_w, spec_w],
        out_specs=[spec_o, spec_o],
        out_shape=[jax.ShapeDtypeStruct((depth, n, d), F32)] * 2,
        compiler_params=_params(("parallel",)))(mem2d, wk, wv)


def _lambda_value(lq1, lk1, lq2, lk2, lam_init):
    return (jnp.exp(jnp.sum(lq1[...] * lk1[...], axis=-1, keepdims=True))
            - jnp.exp(jnp.sum(lq2[...] * lk2[...], axis=-1, keepdims=True)) + lam_init)


def _subln(o, g, lam_init):
    return _rms(o, g) * (1.0 - lam_init)


def _dattn_prompt_kernel(lq1, lk1, lq2, lk2, g_ref, q_ref, k_ref, v_ref, o_ref, *, tq, lam_init):
    qi = pl.program_id(2)
    scale = A_DH ** -0.5
    q = q_ref[0]
    lane = lax.broadcasted_iota(jnp.int32, (tq, A_HD), 1)
    qm = (jnp.where(lane < A_DH, q, 0.0).astype(BF16), jnp.where(lane >= A_DH, q, 0.0).astype(BF16))

    def block(j, carry, masked):
        start = pl.multiple_of(j * tq, tq)
        kb = k_ref[0, pl.ds(start, tq), :].astype(BF16)
        vb = v_ref[0, pl.ds(start, tq), :].astype(BF16)
        out = []
        for mi in range(2):
            m, l, a = carry[3 * mi:3 * mi + 3]
            s = _dot_nt(qm[mi], kb) * scale
            if masked:
                r = lax.broadcasted_iota(jnp.int32, (tq, tq), 0)
                c = lax.broadcasted_iota(jnp.int32, (tq, tq), 1)
                s = jnp.where(c <= r, s, MASK_VALUE)
            m_new = jnp.maximum(m, jnp.max(s, axis=-1, keepdims=True))
            alpha = jnp.exp(m - m_new)
            p = jnp.exp(s - m_new)
            l = alpha * l + jnp.sum(p, axis=-1, keepdims=True)
            a = alpha * a + _dot(p.astype(BF16), vb)
            out += [m_new, l, a]
        return tuple(out)

    init = (jnp.full((tq, 1), NEG_INF, F32), jnp.zeros((tq, 1), F32), jnp.zeros((tq, A_HD), F32)) * 2
    carry = lax.fori_loop(0, qi, lambda j, c: block(j, c, False), init)
    m1, l1, a1, m2, l2, a2 = block(qi, carry, True)
    lam = _lambda_value(lq1, lk1, lq2, lk2, lam_init)
    o = a1 / l1 - lam * (a2 / l2)
    o_ref[0] = _subln(o, g_ref[...], lam_init)


def dattn_prompt(q, k, v, lam_vecs, subln_g, lam_init, *, tq=256):
    b, t, _ = q.shape
    vec = pl.BlockSpec((1, A_DH), lambda bi, h, qi: (0, 0))
    kern = functools.partial(_dattn_prompt_kernel, tq=tq, lam_init=lam_init)
    return pl.pallas_call(
        kern, name="dattn_prompt", grid=(b, A_HEADS, t // tq),
        in_specs=[vec, vec, vec, vec,
                  pl.BlockSpec((1, A_HD), lambda bi, h, qi: (0, 0)),
                  pl.BlockSpec((1, tq, A_HD), lambda bi, h, qi: (bi, qi, h)),
                  pl.BlockSpec((1, t, A_HD), lambda bi, h, qi: (bi, 0, h)),
                  pl.BlockSpec((1, t, A_HD), lambda bi, h, qi: (bi, 0, h))],
        out_specs=pl.BlockSpec((1, tq, A_HD), lambda bi, h, qi: (bi, qi, h)),
        out_shape=jax.ShapeDtypeStruct((b, t, A_QK), F32),
        compiler_params=_params(("parallel", "parallel", "arbitrary")))(
            *[x.reshape(1, A_DH) for x in lam_vecs], subln_g.reshape(1, A_HD), q, k, v)


def _dattn_sample_kernel(pt_ref, lq1, lk1, lq2, lk2, g_ref, q_ref, kc_ref, vc_ref, kn_ref, vn_ref,
                         o_ref, m_scr, l_scr, a_scr, *, n_pages, t_new, lam_init):
    j = pl.program_id(1)
    rows = A_HEADS * 2 * t_new
    scale = A_DH ** -0.5
    q = q_ref[0]
    qt = jnp.concatenate([q] * (rows // t_new), axis=0)
    r_id = lax.broadcasted_iota(jnp.int32, (rows, A_QK), 0)
    c_id = lax.broadcasted_iota(jnp.int32, (rows, A_QK), 1)
    dh_bits, t_bits = A_DH.bit_length() - 1, t_new.bit_length() - 1
    qbig = jnp.where((c_id >> dh_bits) == (r_id >> t_bits), qt, 0.0).astype(BF16)

    @pl.when(j == 0)
    def _():
        m_scr[...] = jnp.full(m_scr.shape, NEG_INF, F32)
        l_scr[...] = jnp.zeros(l_scr.shape, F32)
        a_scr[...] = jnp.zeros(a_scr.shape, F32)

    def update(kb, vb, mask):
        s = _dot_nt(qbig, kb.astype(BF16)) * scale
        if mask is not None:
            s = jnp.where(mask, s, MASK_VALUE)
        m = m_scr[...]
        m_new = jnp.maximum(m, jnp.max(s, axis=-1, keepdims=True))
        alpha = jnp.exp(m - m_new)
        p = jnp.exp(s - m_new)
        l_scr[...] = alpha * l_scr[...] + jnp.sum(p, axis=-1, keepdims=True)
        a_scr[...] = alpha * a_scr[...] + _dot(p.astype(BF16), vb.astype(BF16))
        m_scr[...] = m_new

    @pl.when(j < n_pages)
    def _():
        update(kc_ref[0, 0], vc_ref[0, 0], None)

    @pl.when(j == n_pages)
    def _():
        rr = lax.broadcasted_iota(jnp.int32, (rows, t_new), 0)
        cc = lax.broadcasted_iota(jnp.int32, (rows, t_new), 1)
        update(kn_ref[0], vn_ref[0], cc <= (rr & (t_new - 1)))
        lam = _lambda_value(lq1, lk1, lq2, lk2, lam_init)
        an = a_scr[...] / l_scr[...]
        for h in range(A_HEADS):
            r0 = h * 2 * t_new
            sl = slice(h * A_HD, (h + 1) * A_HD)
            o = an[r0:r0 + t_new, sl] - lam * an[r0 + t_new:r0 + 2 * t_new, sl]
            o_ref[0, :, sl] = _subln(o, g_ref[...], lam_init)


def dattn_sample(q, k_new, v_new, cache_k, cache_v, layer, page_table, lam_vecs, subln_g, lam_init):
    b, t_new, _ = q.shape
    assert t_new & (t_new - 1) == 0
    n_pages = page_table.shape[1]
    pt = page_table.reshape(-1)
    rows = A_HEADS * 2 * t_new
    vec = pl.BlockSpec((1, A_DH), lambda bi, j, pt: (0, 0))
    new = pl.BlockSpec((1, t_new, A_QK), lambda bi, j, pt: (bi, 0, 0))
    page = pl.BlockSpec((1, 1, PAGE_SIZE, A_QK),
                        lambda bi, j, pt: (layer, pt[bi * n_pages + jnp.minimum(j, n_pages - 1)], 0, 0))
    kern = functools.partial(_dattn_sample_kernel, n_pages=n_pages, t_new=t_new, lam_init=lam_init)
    grid_spec = pltpu.PrefetchScalarGridSpec(
        num_scalar_prefetch=1, grid=(b, n_pages + 1),
        in_specs=[vec, vec, vec, vec, pl.BlockSpec((1, A_HD), lambda bi, j, pt: (0, 0)),
                  new, page, page, new, new],
        out_specs=new,
        scratch_shapes=[pltpu.VMEM((rows, 1), F32), pltpu.VMEM((rows, 1), F32),
                        pltpu.VMEM((rows, A_QK), F32)])
    return pl.pallas_call(
        kern, name="dattn_sample", grid_spec=grid_spec, out_shape=jax.ShapeDtypeStruct((b, t_new, A_QK), F32),
        compiler_params=_params(("parallel", "arbitrary")))(
            pt, *[x.reshape(1, A_DH) for x in lam_vecs], subln_g.reshape(1, A_HD),
            q, cache_k, cache_v, k_new, v_new)


def _pool_kernel(p_ref, halo_ref, past_ref, w_ref, sc_ref, o_ref, scr, *, tt, pos0):
    ti = pl.program_id(1)
    scr[0:POOL_MAX, :] = jnp.where(ti == 0, past_ref[0], halo_ref[0])
    scr[POOL_MAX:, :] = p_ref[0]
    pos = pos0 + ti * tt + lax.broadcasted_iota(jnp.int32, (tt, POOL_GDIM), 0)
    for g, w in enumerate(POOL_WINDOWS):
        sl = slice(g * POOL_GDIM, (g + 1) * POOL_GDIM)
        x = scr[POOL_MAX:, sl]
        s = x
        for jj in range(1, w):
            s = s + scr[POOL_MAX - jj:POOL_MAX - jj + tt, sl]
        cnt = jnp.minimum(w, pos + 1).astype(F32)
        d = s / cnt - x
        y = _dot(d.astype(BF16), w_ref[g])
        o_ref[0, :, sl] = y * sc_ref[:, sl]


def pool_mix(p, past16, pos0, pool_w, pool_scale, *, tt=1024):
    b, t, _ = p.shape
    tt = min(tt, t)
    halo_src = p if t >= POOL_MAX else past16
    hb = tt // POOL_MAX
    kern = functools.partial(_pool_kernel, tt=tt, pos0=pos0)
    return pl.pallas_call(
        kern, name="pool_mix", grid=(b, t // tt),
        in_specs=[pl.BlockSpec((1, tt, POOL_DIM), lambda bi, ti: (bi, ti, 0)),
                  pl.BlockSpec((1, POOL_MAX, POOL_DIM), lambda bi, ti: (bi, jnp.maximum(ti * hb - 1, 0), 0)),
                  pl.BlockSpec((1, POOL_MAX, POOL_DIM), lambda bi, ti: (bi, 0, 0)),
                  pl.BlockSpec(pool_w.shape, lambda bi, ti: (0, 0, 0)),
                  pl.BlockSpec((1, POOL_DIM), lambda bi, ti: (0, 0))],
        out_specs=pl.BlockSpec((1, tt, POOL_DIM), lambda bi, ti: (bi, ti, 0)),
        out_shape=jax.ShapeDtypeStruct((b, t, POOL_DIM), F32),
        scratch_shapes=[pltpu.VMEM((POOL_MAX + tt, POOL_DIM), F32)],
        compiler_params=_params(("parallel", "arbitrary")))(
            p, halo_src, past16, pool_w, pool_scale.reshape(1, POOL_DIM))


CONV_HALO = 8


def _conv_kernel(u_ref, halo_ref, past_ref, b_ref, cw_ref, o_ref, scr, *, tt):
    ti = pl.program_id(1)
    scr[0:CONV_HALO, :] = jnp.where(ti == 0, past_ref[0], halo_ref[0])
    scr[CONV_HALO:, :] = u_ref[0]
    y = None
    for jj in range(CONV_W):
        off = CONV_HALO - (CONV_W - 1) + jj
        term = cw_ref[jj:jj + 1, :] * scr[off:off + tt, :]
        y = term if y is None else y + term
    o_ref[0] = b_ref[0] * y


def short_conv(u, bgate, past8, conv_w, *, tt=1024):
    b, t, d = u.shape
    tt = min(tt, t)
    hb = tt // CONV_HALO
    tile = pl.BlockSpec((1, tt, d), lambda bi, ti: (bi, ti, 0))
    kern = functools.partial(_conv_kernel, tt=tt)
    return pl.pallas_call(
        kern, name="short_conv", grid=(b, t // tt),
        in_specs=[tile,
                  pl.BlockSpec((1, CONV_HALO, d), lambda bi, ti: (bi, jnp.maximum(ti * hb - 1, 0), 0)),
                  pl.BlockSpec((1, CONV_HALO, d), lambda bi, ti: (bi, 0, 0)),
                  tile,
                  pl.BlockSpec((CONV_W, d), lambda bi, ti: (0, 0))],
        out_specs=tile,
        out_shape=jax.ShapeDtypeStruct((b, t, d), F32),
        scratch_shapes=[pltpu.VMEM((CONV_HALO + tt, d), F32)],
        compiler_params=_params(("parallel", "arbitrary")))(u, u, past8, bgate, conv_w)


def _memattn_kernel(q_ref, k_ref, v_ref, o_ref):
    scale = MEM_DH ** -0.5
    for h in range(MEM_HEADS):
        sl = slice(h * MEM_DH, (h + 1) * MEM_DH)
        qh = q_ref[0, :, sl].astype(BF16)
        kh = k_ref[0, 0, :, sl].astype(BF16)
        vh = v_ref[0, 0, :, sl].astype(BF16)
        s = _dot_nt(qh, kh) * scale
        s = s - jnp.max(s, axis=-1, keepdims=True)
        p = jnp.exp(s)
        p = p / jnp.sum(p, axis=-1, keepdims=True)
        o_ref[0, :, sl] = _dot(p.astype(BF16), vh)


def mem_attn(q, mk, mv, layer, *, tq=512):
    b, t, d = q.shape
    tq = min(tq, t)
    tile = pl.BlockSpec((1, tq, d), lambda bi, ti: (bi, ti, 0))
    mem = pl.BlockSpec((1, 1, N_MEM, d), lambda bi, ti: (layer, bi, 0, 0))
    return pl.pallas_call(
        _memattn_kernel, name="mem_attn", grid=(b, t // tq),
        in_specs=[tile, mem, mem], out_specs=tile,
        out_shape=jax.ShapeDtypeStruct((b, t, d), F32),
        compiler_params=_params(("parallel", "parallel")))(q, mk, mv)


def _peer_select_kernel(x_ref, g_ref, wqt_ref, sk_ref, ht_ref, s1_ref, s2_ref, e1_ref, e2_ref, thr_ref,
                        qt_scr, v_scr, *, tt):
    h = _rms(x_ref[...], g_ref[...])
    ht = h.T.astype(BF16)
    ht_ref[...] = ht
    qt_scr[...] = _dot(wqt_ref[...], ht)

    def top_values(s, slot):
        for k in range(PEER_TOPK):
            m = jnp.max(s, axis=0, keepdims=True)
            v_scr[slot, k:k + 1, :] = m
            if k + 1 < PEER_TOPK:
                s = jnp.where(s == m, NEG_INF, s)

    def head(hh, carry):
        base = pl.multiple_of(hh * 2 * PEER_NKEYS, 2 * PEER_NKEYS)
        q1 = qt_scr[pl.ds(base, PEER_NKEYS), :].astype(BF16)
        q2 = qt_scr[pl.ds(base + PEER_NKEYS, PEER_NKEYS), :].astype(BF16)
        s1 = _dot(sk_ref[2 * hh], q1)
        s2 = _dot(sk_ref[2 * hh + 1], q2)
        top_values(s1, 0)
        top_values(s2, 1)
        v1 = v_scr[0]
        v2 = v_scr[1]
        half = PEER_TOPK // 2
        cands = [v1[a:a + 1, :] + v2[0:half, :] for a in range(half)]
        cands.append(v1[0:1, :] + v2[half:, :])
        cands.append(v1[half:, :] + v2[0:1, :])
        tops = []
        for k in range(PEER_TOPK):
            m = functools.reduce(jnp.maximum, cands)
            m = jnp.max(m, axis=0, keepdims=True)
            tops.append(m)
            if k + 1 < PEER_TOPK:
                cands = [jnp.where(c == m, NEG_INF, c) for c in cands]
        z = functools.reduce(lambda a, b: a + b, [jnp.exp(t - tops[0]) for t in tops])
        row = pl.multiple_of(hh * PEER_NKEYS, PEER_NKEYS)
        s1_ref[pl.ds(row, PEER_NKEYS), :] = s1
        s2_ref[pl.ds(row, PEER_NKEYS), :] = s2
        e1_ref[pl.ds(row, PEER_NKEYS), :] = jnp.exp(s1 - v1[0:1, :]) / z
        e2_ref[pl.ds(row, PEER_NKEYS), :] = jnp.exp(s2 - v2[0:1, :])
        thr_ref[pl.ds(hh, 1), :] = tops[PEER_TOPK - 1]
        return carry

    lax.fori_loop(0, PEER_HEADS, head, 0)


def peer_select(x, gain, wq_t, subkeys, *, tt=256):
    n, d = x.shape
    rows = PEER_HEADS * PEER_NKEYS
    col = lambda r: pl.BlockSpec((r, tt), lambda i: (0, i))
    kern = functools.partial(_peer_select_kernel, tt=tt)
    return pl.pallas_call(
        kern, name="peer_select", grid=(n // tt,),
        in_specs=[pl.BlockSpec((tt, d), lambda i: (i, 0)),
                  pl.BlockSpec((1, d), lambda i: (0, 0)),
                  pl.BlockSpec(wq_t.shape, lambda i: (0, 0)),
                  pl.BlockSpec(subkeys.shape, lambda i: (0, 0, 0))],
        out_specs=[col(d), col(rows), col(rows), col(rows), col(rows), col(PEER_HEADS)],
        out_shape=[jax.ShapeDtypeStruct((d, n), BF16)] + [jax.ShapeDtypeStruct((rows, n), F32)] * 4
        + [jax.ShapeDtypeStruct((PEER_HEADS, n), F32)],
        scratch_shapes=[pltpu.VMEM((2 * rows, tt), F32), pltpu.VMEM((2, PEER_TOPK, tt), F32)],
        compiler_params=_params(("parallel",)))(x, gain.reshape(1, d), wq_t, subkeys)


PEER_CHUNK = 512


def _peer_dense_kernel(x_ref, ht_ref, s1_ref, s2_ref, e1_ref, e2_ref, thr_ref, u_ref, vt_ref, o_ref,
                       acc_scr, a_scr, p_scr):
    c = pl.program_id(1)
    a_scr[...] = _dot(u_ref[...], ht_ref[...])
    rows_per_chunk = PEER_CHUNK // PEER_NKEYS

    def key_row(r, carry):
        i = c * rows_per_chunk + r
        w = None
        for hh in range(PEER_HEADS):
            s1 = s1_ref[pl.ds(hh * PEER_NKEYS + i, 1), :]
            e1 = e1_ref[pl.ds(hh * PEER_NKEYS + i, 1), :]
            s2 = s2_ref[hh * PEER_NKEYS:(hh + 1) * PEER_NKEYS, :]
            e2 = e2_ref[hh * PEER_NKEYS:(hh + 1) * PEER_NKEYS, :]
            sel = jnp.where(s1 + s2 >= thr_ref[hh:hh + 1, :], e1 * e2, 0.0)
            w = sel if w is None else w + sel
        off = pl.multiple_of(r * PEER_NKEYS, PEER_NKEYS)
        a = a_scr[pl.ds(off, PEER_NKEYS), :]
        p_scr[pl.ds(off, PEER_NKEYS), :] = (_gelu(a) * w).astype(BF16)
        return carry

    lax.fori_loop(0, rows_per_chunk, key_row, 0)
    contrib = _dot(vt_ref[...], p_scr[...])

    @pl.when(c == 0)
    def _():
        acc_scr[...] = contrib

    @pl.when(c > 0)
    def _():
        acc_scr[...] += contrib

    @pl.when(c == pl.num_programs(1) - 1)
    def _():
        o_ref[...] = x_ref[...] + acc_scr[...].T


def peer_dense(x, ht, s1, s2, e1, e2, thr, u_tab, v_tab_t, *, tt=512):
    n, d = x.shape
    tt = min(tt, n)
    assert n % tt == 0
    rows = PEER_HEADS * PEER_NKEYS
    col = lambda r: pl.BlockSpec((r, tt), lambda i, c: (0, i))
    return pl.pallas_call(
        _peer_dense_kernel, name="peer_dense", grid=(n // tt, PEER_EXPERTS // PEER_CHUNK),
        in_specs=[pl.BlockSpec((tt, d), lambda i, c: (i, 0)),
                  col(d), col(rows), col(rows), col(rows), col(rows), col(PEER_HEADS),
                  pl.BlockSpec((PEER_CHUNK, d), lambda i, c: (c, 0)),
                  pl.BlockSpec((d, PEER_CHUNK), lambda i, c: (0, c))],
        out_specs=pl.BlockSpec((tt, d), lambda i, c: (i, 0)),
        out_shape=jax.ShapeDtypeStruct((n, d), F32),
        scratch_shapes=[pltpu.VMEM((d, tt), F32), pltpu.VMEM((PEER_CHUNK, tt), F32),
                        pltpu.VMEM((PEER_CHUNK, tt), BF16)],
        compiler_params=_params(("parallel", "arbitrary")))(x, ht, s1, s2, e1, e2, thr, u_tab, v_tab_t)


def _norm_kernel(x_ref, g_ref, o_ref):
    o_ref[...] = _rms(x_ref[...], g_ref[...])


def final_norm(x, gain, *, tn=512):
    n, d = x.shape
    tn = min(tn, n)
    assert n % tn == 0
    tile = pl.BlockSpec((tn, d), lambda i: (i, 0))
    return pl.pallas_call(
        _norm_kernel, name="final_norm", grid=(n // tn,),
        in_specs=[tile, pl.BlockSpec((1, d), lambda i: (0, 0))], out_specs=tile,
        out_shape=jax.ShapeDtypeStruct((n, d), F32),
        compiler_params=_params(("parallel",)))(x, gain.reshape(1, d))


def _trunk(x, pos0, mem_k, mem_v, pool_past, conv_past, cache_k, cache_v, page_table, W):
    b, t, d = x.shape
    n = b * t
    x = x.reshape(n, d)
    new_k, new_v, new_pool, new_conv = [], [], [], []
    for l in range(DEPTH):
        i = l // 2
        if l % 2 == 0:
            lam_init = 0.8 - 0.6 * math.exp(-0.3 * l)
            q, k, v, p = fused_linear([x], [W["w_in_ab"][i]], name="mix_in_ab", gain=W["norm_mix"][l],
                                      splits=[A_QK] * 4)
            q3, k3, v3, p3 = (a.reshape(b, t, A_QK) for a in (q, k, v, p))
            lam_vecs = [W[nm][i] for nm in ("lambda_q1", "lambda_k1", "lambda_q2", "lambda_k2")]
            if page_table is None:
                o = dattn_prompt(q3, k3, v3, lam_vecs, W["subln_g"][i], lam_init)
            else:
                o = dattn_sample(q3, k3, v3, cache_k, cache_v, i, page_table, lam_vecs,
                                 W["subln_g"][i], lam_init)
            past = pool_past[i]
            past16 = jnp.pad(past, ((0, 0), (1, 0), (0, 0)))
            pooled = pool_mix(p3, past16, pos0, W["pool_w"][i], W["pool_scale"][i])
            x = fused_linear([o.reshape(n, A_QK), pooled.reshape(n, POOL_DIM)],
                             [W["w_out_ab"][i][:A_QK], W["w_out_ab"][i][A_QK:]], name="mix_out_ab", res=x)
            new_k.append(k3.reshape(b, t, A_HEADS, A_HD))
            new_v.append(v3.reshape(b, t, A_HEADS, A_HD))
            new_pool.append(jnp.concatenate([past, p3], axis=1)[:, -(POOL_MAX - 1):])
        else:
            bg, u = fused_linear([x], [W["w_in_c"][i]], name="mix_in_c", gain=W["norm_mix"][l], gate=True)
            u3 = u.reshape(b, t, d)
            past = conv_past[i]
            past8 = jnp.pad(past, ((0, 0), (CONV_HALO - (CONV_W - 1), 0), (0, 0)))
            z = short_conv(u3, bg.reshape(b, t, d), past8, W["conv_w"][i])
            x = fused_linear([z.reshape(n, d)], [W["w_out_c"][i]], name="mix_out_c", res=x)
            new_conv.append(jnp.concatenate([past, u3], axis=1)[:, -(CONV_W - 1):])
        qm = fused_linear([x], [W["w_mq"][l]], name="mem_q", gain=W["norm_mem"][l])
        om = mem_attn(qm.reshape(b, t, d), mem_k, mem_v, l)
        x = fused_linear([om.reshape(n, d)], [W["w_mo"][l]], name="mem_o", res=x)
        sel = peer_select(x, W["norm_ffn"][l], W["peer_wq_t"][l], W["peer_subkeys"][l])
        x = peer_dense(x, *sel, W["peer_u"][l], W["peer_v_t"][l])
    y = final_norm(x, W["norm_final"]).reshape(b, t, d)
    return y, jnp.stack(new_k), jnp.stack(new_v), jnp.stack(new_pool), jnp.stack(new_conv)


def _bf(w):
    return w.astype(BF16)


def _prepare_weights(**w):
    out = dict(w)
    for name in ("w_in_ab", "w_out_ab", "pool_w", "w_in_c", "w_out_c", "w_mq", "w_mo", "peer_u"):
        out[name] = _bf(w[name])
    out["peer_wq_t"] = _bf(jnp.swapaxes(w["peer_wq"], 1, 2))
    out["peer_subkeys"] = _bf(w["peer_subkeys"].reshape(DEPTH, PEER_HEADS * 2, PEER_NKEYS, PEER_NKEYS))
    out["peer_v_t"] = _bf(jnp.swapaxes(w["peer_v"], 1, 2))
    return out


def kernel(x_prompt,x_sample, cache_attn_k, cache_attn_v, state_pool, state_conv, cache_mem_k, cache_mem_v, page_table, mem_prompt, norm_mix, norm_mem, norm_ffn, norm_final, w_in_ab, w_out_ab, lambda_q1, lambda_k1, lambda_q2, lambda_k2, subln_g, pool_w, pool_scale, w_in_c, conv_w, w_out_c, w_mq, w_mk, w_mv, w_mo, peer_wq, peer_subkeys, peer_u, peer_v):
    bf = _bf
    W = _prepare_weights(
        norm_mix=norm_mix, norm_mem=norm_mem, norm_ffn=norm_ffn, norm_final=norm_final, w_in_ab=w_in_ab,
        w_out_ab=w_out_ab, lambda_q1=lambda_q1, lambda_k1=lambda_k1, lambda_q2=lambda_q2, lambda_k2=lambda_k2,
        subln_g=subln_g, pool_w=pool_w, pool_scale=pool_scale, w_in_c=w_in_c, conv_w=conv_w, w_out_c=w_out_c,
        w_mq=w_mq, w_mo=w_mo, peer_wq=peer_wq, peer_subkeys=peer_subkeys, peer_u=peer_u, peer_v=peer_v)
    bp, sp, d = x_prompt.shape
    bs = x_sample.shape[0]
    mk_p, mv_p = mem_project(mem_prompt.reshape(bp * N_MEM, d), bf(w_mk), bf(w_mv))
    mk_p = mk_p.reshape(DEPTH, bp, N_MEM, d)
    mv_p = mv_p.reshape(DEPTH, bp, N_MEM, d)
    pool0 = jnp.zeros((w_in_ab.shape[0], bp, POOL_MAX - 1, POOL_DIM), F32)
    conv0 = jnp.zeros((w_in_c.shape[0], bp, CONV_W - 1, d), F32)
    y_p, k_p, v_p, pool_p, conv_p = _trunk(x_prompt, 0, mk_p, mv_p, pool0, conv0, None, None, None, W)
    past_len = page_table.shape[1] * cache_attn_k.shape[2]
    n_layers, n_pool = cache_attn_k.shape[:2]
    ck = cache_attn_k.reshape(n_layers, n_pool, PAGE_SIZE, A_QK)
    cv = cache_attn_v.reshape(n_layers, n_pool, PAGE_SIZE, A_QK)
    y_s, k_s, v_s, pool_s, conv_s = _trunk(
        x_sample, past_len, cache_mem_k.reshape(DEPTH, bs, N_MEM, d), cache_mem_v.reshape(DEPTH, bs, N_MEM, d),
        state_pool, state_conv, ck, cv, page_table, W)
    mshape = (DEPTH, bp, N_MEM, MEM_HEADS, MEM_DH)
    return (y_p, y_s, k_p, v_p, pool_p, conv_p, mk_p.reshape(mshape), mv_p.reshape(mshape),
            k_s, v_s, pool_s, conv_s)
```

```python
import functools
import math

import jax
import jax.numpy as jnp
from jax import lax
from jax.experimental import pallas as pl
from jax.experimental.pallas import tpu as pltpu

F32 = jnp.float32
BF16 = jnp.bfloat16

D_MODEL = 1024
DEPTH = 4
EPS = 1e-5
A_HEADS = 4
A_DH = 64
A_HD = 2 * A_DH
A_QK = A_HEADS * A_HD
POOL_WINDOWS = (2, 4, 8, 16)
POOL_GDIM = 128
POOL_DIM = 512
POOL_MAX = 16
CONV_W = 3
N_MEM = 256
MEM_HEADS = 4
MEM_DH = 256
PEER_HEADS = 8
PEER_NKEYS = 128
PEER_EXPERTS = PEER_NKEYS * PEER_NKEYS
PEER_TOPK = 16
PAGE_SIZE = 128

SUBLANES = 8
MXU_WIDTH = 256
VMEM_LIMIT = 56 * 1024 * 1024
NEG_INF = float("-inf")
MASK_VALUE = float(jnp.finfo(jnp.float32).min)
LOG2E = 1.4426950408889634


def _params(sem, vmem=VMEM_LIMIT):
    return pltpu.CompilerParams(dimension_semantics=sem, vmem_limit_bytes=vmem)


def _rms(x, g):
    return x * lax.rsqrt(jnp.mean(x * x, axis=-1, keepdims=True) + EPS) * g


def _gelu(a):
    return 0.5 * a * (1.0 + lax.erf(a * (2.0 ** -0.5)))


def _dot(a, b):
    return jnp.dot(a, b, preferred_element_type=F32)


def _dot_nt(a, b):
    return lax.dot_general(a, b, (((1,), (1,)), ((), ())), preferred_element_type=F32)


def _linear_kernel(*refs, n_in, has_norm, has_res, splits, gate):
    a_refs = refs[:n_in]
    w_refs = refs[n_in:2 * n_in]
    idx = 2 * n_in
    g_ref = refs[idx] if has_norm else None
    idx += int(has_norm)
    r_ref = refs[idx] if has_res else None
    idx += int(has_res)
    out_refs = refs[idx:]
    acc = None
    for a_ref, w_ref in zip(a_refs, w_refs):
        a = a_ref[...]
        if has_norm:
            a = _rms(a, g_ref[...])
        d = _dot(a.astype(BF16), w_ref[...])
        acc = d if acc is None else acc + d
    if has_res:
        acc = acc + r_ref[...]
    if gate:
        d3 = acc.shape[1] // 3
        out_refs[0][...] = acc[:, :d3]
        out_refs[1][...] = acc[:, d3:2 * d3] * acc[:, 2 * d3:]
    else:
        off = 0
        for o_ref, s in zip(out_refs, splits):
            o_ref[...] = acc[:, off:off + s]
            off += s


def fused_linear(a_list, w_list, *, name, gain=None, res=None, splits=None, gate=False, tn=512):
    n = a_list[0].shape[0]
    m = w_list[0].shape[1]
    tn = min(tn, n)
    assert n % tn == 0
    if gate:
        out_cols = [m // 3, m // 3]
    else:
        out_cols = list(splits) if splits is not None else [m]
        assert sum(out_cols) == m
    in_specs, args = [], []
    for a in a_list:
        in_specs.append(pl.BlockSpec((tn, a.shape[1]), lambda i: (i, 0)))
        args.append(a)
    for w in w_list:
        in_specs.append(pl.BlockSpec(w.shape, lambda i: (0, 0)))
        args.append(w)
    if gain is not None:
        in_specs.append(pl.BlockSpec((1, gain.shape[-1]), lambda i: (0, 0)))
        args.append(gain.reshape(1, -1))
    if res is not None:
        in_specs.append(pl.BlockSpec((tn, m), lambda i: (i, 0)))
        args.append(res)
    out_shape = [jax.ShapeDtypeStruct((n, c), F32) for c in out_cols]
    out_specs = [pl.BlockSpec((tn, c), lambda i: (i, 0)) for c in out_cols]
    kern = functools.partial(_linear_kernel, n_in=len(a_list), has_norm=gain is not None,
                             has_res=res is not None, splits=tuple(out_cols), gate=gate)
    outs = pl.pallas_call(
        kern, name=name, grid=(n // tn,), in_specs=in_specs, out_specs=out_specs, out_shape=out_shape,
        compiler_params=_params(("parallel",)))(*args)
    return outs if len(outs) > 1 else outs[0]


def _memproj_kernel(a_ref, wk_ref, wv_ref, ok_ref, ov_ref):
    a = a_ref[...].astype(BF16)
    ok_ref[0] = _dot(a, wk_ref[0])
    ov_ref[0] = _dot(a, wv_ref[0])


def mem_project(mem2d, wk, wv):
    n, d = mem2d.shape
    depth = wk.shape[0]
    spec_w = pl.BlockSpec((1, d, d), lambda l: (l, 0, 0))
    spec_o = pl.BlockSpec((1, n, d), lambda l: (l, 0, 0))
    return pl.pallas_call(
        _memproj_kernel, name="mem_project", grid=(depth,),
        in_specs=[pl.BlockSpec((n, d), lambda l: (0, 0)), spec_w, spec_w],
        out_specs=[spec_o, spec_o],
        out_shape=[jax.ShapeDtypeStruct((depth, n, d), F32)] * 2,
        compiler_params=_params(("parallel",)))(mem2d, wk, wv)


def _lambda_value(lq1, lk1, lq2, lk2, lam_init):
    return (jnp.exp(jnp.sum(lq1[...] * lk1[...], axis=-1, keepdims=True))
            - jnp.exp(jnp.sum(lq2[...] * lk2[...], axis=-1, keepdims=True)) + lam_init)


def _subln(o, g, lam_init):
    return _rms(o, g) * (1.0 - lam_init)


def _dattn_prompt_kernel(lq1, lk1, lq2, lk2, g_ref, q_ref, k_ref, v_ref, o_ref, kb_scr, vt_scr, *,
                         tq, lam_init):
    qi = pl.program_id(2)
    nblk = kb_scr.shape[0]

    @pl.when(qi == 0)
    def _():
        for jb in range(nblk):
            kb_scr[jb] = k_ref[0, jb * tq:(jb + 1) * tq, :].astype(BF16)
            vt_scr[jb] = v_ref[0, jb * tq:(jb + 1) * tq, :].T.astype(BF16)

    qt = (q_ref[0] * (A_DH ** -0.5 * LOG2E)).T
    sub = lax.broadcasted_iota(jnp.int32, (A_HD, tq), 0)
    qz = (jnp.where(sub < A_DH, qt, 0.0).astype(BF16), jnp.where(sub >= A_DH, qt, 0.0).astype(BF16))

    def block(j, carry, masked):
        kb = kb_scr[j]
        vt = vt_scr[j]
        out = []
        for mi in range(2):
            m, l, a = carry[3 * mi:3 * mi + 3]
            s = _dot(kb, qz[mi])
            if masked:
                kk = lax.broadcasted_iota(jnp.int32, (tq, tq), 0)
                qq = lax.broadcasted_iota(jnp.int32, (tq, tq), 1)
                s = jnp.where(kk <= qq, s, MASK_VALUE)
            m_new = jnp.maximum(m, jnp.max(s, axis=0, keepdims=True))
            alpha = jnp.exp2(m - m_new)
            p = jnp.exp2(s - m_new)
            l = alpha * l + jnp.sum(p, axis=0, keepdims=True)
            a = alpha * a + _dot(vt, p.astype(BF16))
            out += [m_new, l, a]
        return tuple(out)

    init = (jnp.full((1, tq), NEG_INF, F32), jnp.zeros((1, tq), F32), jnp.zeros((A_HD, tq), F32)) * 2
    carry = lax.fori_loop(0, qi, lambda j, c: block(j, c, False), init)
    m1, l1, a1, m2, l2, a2 = block(qi, carry, True)
    lam = _lambda_value(lq1, lk1, lq2, lk2, lam_init)
    ot = a1 / l1 - lam * (a2 / l2)
    o_ref[0] = _subln(ot.T, g_ref[...], lam_init)


def dattn_prompt(q, k, v, lam_vecs, subln_g, lam_init, *, tq=512):
    b, t, _ = q.shape
    tq = min(tq, t)
    assert t % tq == 0
    vec = pl.BlockSpec((1, A_DH), lambda bi, h, qi: (0, 0))
    kern = functools.partial(_dattn_prompt_kernel, tq=tq, lam_init=lam_init)
    return pl.pallas_call(
        kern, name="dattn_prompt", grid=(b, A_HEADS, t // tq),
        in_specs=[vec, vec, vec, vec,
                  pl.BlockSpec((1, A_HD), lambda bi, h, qi: (0, 0)),
                  pl.BlockSpec((1, tq, A_HD), lambda bi, h, qi: (bi, qi, h)),
                  pl.BlockSpec((1, t, A_HD), lambda bi, h, qi: (bi, 0, h)),
                  pl.BlockSpec((1, t, A_HD), lambda bi, h, qi: (bi, 0, h))],
        out_specs=pl.BlockSpec((1, tq, A_HD), lambda bi, h, qi: (bi, qi, h)),
        out_shape=jax.ShapeDtypeStruct((b, t, A_QK), F32),
        scratch_shapes=[pltpu.VMEM((t // tq, tq, A_HD), BF16), pltpu.VMEM((t // tq, A_HD, tq), BF16)],
        compiler_params=_params(("parallel", "parallel", "arbitrary")))(
            *[x.reshape(1, A_DH) for x in lam_vecs], subln_g.reshape(1, A_HD), q, k, v)


def _dattn_sample_kernel(pt_ref, lq1, lk1, lq2, lk2, g_ref, q_ref, kn_ref, vn_ref, *rest,
                         n_pages, t_new, lam_init):
    k_refs = rest[:n_pages]
    v_refs = rest[n_pages:2 * n_pages]
    o_ref = rest[2 * n_pages]
    rows = A_HEADS * 2 * t_new
    hbits = A_HEADS.bit_length() - 1
    tbits = t_new.bit_length() - 1
    q = q_ref[0] * (A_DH ** -0.5)
    lane = lax.broadcasted_iota(jnp.int32, (t_new, A_HD), 1)
    parts = []
    for h in range(A_HEADS):
        qh = q[:, h * A_HD:(h + 1) * A_HD]
        parts += [jnp.where(lane < A_DH, qh, 0.0), jnp.where(lane >= A_DH, qh, 0.0)]
    qz = jnp.concatenate(parts, axis=0).astype(BF16)

    def bias(ncols, causal):
        rr = lax.broadcasted_iota(jnp.int32, (rows, ncols), 0)
        cc = lax.broadcasted_iota(jnp.int32, (rows, ncols), 1)
        ok = (cc & (A_HEADS - 1)) == (rr >> (tbits + 1))
        if causal:
            ok = ok & ((cc >> hbits) <= (rr & (t_new - 1)))
        return jnp.where(ok, 0.0, NEG_INF)

    page_bias = bias(PAGE_SIZE * A_HEADS, False)
    s_pages = [_dot_nt(qz, k_ref[0, 0].astype(BF16)) + page_bias for k_ref in k_refs]
    s_new = _dot_nt(qz, kn_ref[0].astype(BF16)) + bias(t_new * A_HEADS, True)
    m = jnp.max(functools.reduce(jnp.maximum, s_pages), axis=-1, keepdims=True)
    m = jnp.maximum(m, jnp.max(s_new, axis=-1, keepdims=True))
    p_new = jnp.exp(s_new - m)
    l = jnp.sum(p_new, axis=-1, keepdims=True)
    acc = _dot(p_new.astype(BF16), vn_ref[0].astype(BF16))
    for s, v_ref in zip(s_pages, v_refs):
        p = jnp.exp(s - m)
        l = l + jnp.sum(p, axis=-1, keepdims=True)
        acc = acc + _dot(p.astype(BF16), v_ref[0, 0].astype(BF16))
    acc = acc / l
    lam = _lambda_value(lq1, lk1, lq2, lk2, lam_init)
    for h in range(A_HEADS):
        r0 = h * 2 * t_new
        o = acc[r0:r0 + t_new] - lam * acc[r0 + t_new:r0 + 2 * t_new]
        o_ref[0, :, h * A_HD:(h + 1) * A_HD] = _subln(o, g_ref[...], lam_init)


def dattn_sample(q, k_new, v_new, cache_k, cache_v, layer, page_table, lam_vecs, subln_g, lam_init):
    b, t_new, _ = q.shape
    assert t_new & (t_new - 1) == 0
    n_pages = page_table.shape[1]
    pt = page_table.reshape(-1)
    vec = pl.BlockSpec((1, A_DH), lambda bi, pt: (0, 0))
    qspec = pl.BlockSpec((1, t_new, A_QK), lambda bi, pt: (bi, 0, 0))
    nspec = pl.BlockSpec((1, t_new * A_HEADS, A_HD), lambda bi, pt: (bi, 0, 0))

    def page(r):
        return pl.BlockSpec((1, 1, PAGE_SIZE * A_HEADS, A_HD),
                            lambda bi, pt: (layer, pt[bi * n_pages + r], 0, 0))

    pages = [page(r) for r in range(n_pages)]
    kern = functools.partial(_dattn_sample_kernel, n_pages=n_pages, t_new=t_new, lam_init=lam_init)
    grid_spec = pltpu.PrefetchScalarGridSpec(
        num_scalar_prefetch=1, grid=(b,),
        in_specs=[vec, vec, vec, vec, pl.BlockSpec((1, A_HD), lambda bi, pt: (0, 0)),
                  qspec, nspec, nspec] + pages + pages,
        out_specs=qspec)
    return pl.pallas_call(
        kern, name="dattn_sample", grid_spec=grid_spec,
        out_shape=jax.ShapeDtypeStruct((b, t_new, A_QK), F32),
        compiler_params=_params(("parallel",)))(
            pt, *[x.reshape(1, A_DH) for x in lam_vecs], subln_g.reshape(1, A_HD), q,
            k_new.reshape(b, t_new * A_HEADS, A_HD), v_new.reshape(b, t_new * A_HEADS, A_HD),
            *([cache_k] * n_pages), *([cache_v] * n_pages))


def _pool_kernel(p_ref, halo_ref, past_ref, w_ref, sc_ref, o_ref, scr, *, tt, pos0):
    ti = pl.program_id(1)
    scr[0:POOL_MAX, :] = jnp.where(ti == 0, past_ref[0], halo_ref[0])
    scr[POOL_MAX:, :] = p_ref[0]
    pos = pos0 + ti * tt + lax.broadcasted_iota(jnp.int32, (tt, POOL_GDIM), 0)
    for g, w in enumerate(POOL_WINDOWS):
        sl = slice(g * POOL_GDIM, (g + 1) * POOL_GDIM)
        x = scr[POOL_MAX:, sl]
        s = x
        for jj in range(1, w):
            s = s + scr[POOL_MAX - jj:POOL_MAX - jj + tt, sl]
        cnt = jnp.minimum(w, pos + 1).astype(F32)
        d = s / cnt - x
        y = _dot(d.astype(BF16), w_ref[g])
        o_ref[0, :, sl] = y * sc_ref[:, sl]


def pool_mix(p, past16, pos0, pool_w, pool_scale, *, tt=1024):
    b, t, _ = p.shape
    tt = min(tt, t)
    halo_src = p if t >= POOL_MAX else past16
    hb = tt // POOL_MAX
    kern = functools.partial(_pool_kernel, tt=tt, pos0=pos0)
    return pl.pallas_call(
        kern, name="pool_mix", grid=(b, t // tt),
        in_specs=[pl.BlockSpec((1, tt, POOL_DIM), lambda bi, ti: (bi, ti, 0)),
                  pl.BlockSpec((1, POOL_MAX, POOL_DIM), lambda bi, ti: (bi, jnp.maximum(ti * hb - 1, 0), 0)),
                  pl.BlockSpec((1, POOL_MAX, POOL_DIM), lambda bi, ti: (bi, 0, 0)),
                  pl.BlockSpec(pool_w.shape, lambda bi, ti: (0, 0, 0)),
                  pl.BlockSpec((1, POOL_DIM), lambda bi, ti: (0, 0))],
        out_specs=pl.BlockSpec((1, tt, POOL_DIM), lambda bi, ti: (bi, ti, 0)),
        out_shape=jax.ShapeDtypeStruct((b, t, POOL_DIM), F32),
        scratch_shapes=[pltpu.VMEM((POOL_MAX + tt, POOL_DIM), F32)],
        compiler_params=_params(("parallel", "arbitrary")))(
            p, halo_src, past16, pool_w, pool_scale.reshape(1, POOL_DIM))


CONV_HALO = 8


def _conv_kernel(u_ref, halo_ref, past_ref, b_ref, cw_ref, o_ref, scr, *, tt):
    ti = pl.program_id(1)
    scr[0:CONV_HALO, :] = jnp.where(ti == 0, past_ref[0], halo_ref[0])
    scr[CONV_HALO:, :] = u_ref[0]
    y = None
    for jj in range(CONV_W):
        off = CONV_HALO - (CONV_W - 1) + jj
        term = cw_ref[jj:jj + 1, :] * scr[off:off + tt, :]
        y = term if y is None else y + term
    o_ref[0] = b_ref[0] * y


def short_conv(u, bgate, past8, conv_w, *, tt=1024):
    b, t, d = u.shape
    tt = min(tt, t)
    hb = tt // CONV_HALO
    tile = pl.BlockSpec((1, tt, d), lambda bi, ti: (bi, ti, 0))
    kern = functools.partial(_conv_kernel, tt=tt)
    return pl.pallas_call(
        kern, name="short_conv", grid=(b, t // tt),
        in_specs=[tile,
                  pl.BlockSpec((1, CONV_HALO, d), lambda bi, ti: (bi, jnp.maximum(ti * hb - 1, 0), 0)),
                  pl.BlockSpec((1, CONV_HALO, d), lambda bi, ti: (bi, 0, 0)),
                  tile,
                  pl.BlockSpec((CONV_W, d), lambda bi, ti: (0, 0))],
        out_specs=tile,
        out_shape=jax.ShapeDtypeStruct((b, t, d), F32),
        scratch_shapes=[pltpu.VMEM((CONV_HALO + tt, d), F32)],
        compiler_params=_params(("parallel", "arbitrary")))(u, u, past8, bgate, conv_w)


def _memattn_kernel(q_ref, k_ref, v_ref, o_ref, *, head_axis):
    scale = MEM_DH ** -0.5
    for h in range(MEM_HEADS):
        sl = slice(h * MEM_DH, (h + 1) * MEM_DH)
        qh = (q_ref[0, :, sl] * scale).astype(BF16)
        if head_axis:
            kh = k_ref[0, 0, :, h, :].astype(BF16)
            vh = v_ref[0, 0, :, h, :].astype(BF16)
        else:
            kh = k_ref[0, 0, :, sl].astype(BF16)
            vh = v_ref[0, 0, :, sl].astype(BF16)
        s = _dot_nt(qh, kh)
        s = s - jnp.max(s, axis=-1, keepdims=True)
        p = jnp.exp(s)
        p = p / jnp.sum(p, axis=-1, keepdims=True)
        o_ref[0, :, sl] = _dot(p.astype(BF16), vh)


def mem_attn(q, mk, mv, layer, *, tq=512):
    b, t, d = q.shape
    tq = min(tq, t)
    head_axis = mk.ndim == 5
    tile = pl.BlockSpec((1, tq, d), lambda bi, ti: (bi, ti, 0))
    if head_axis:
        mem = pl.BlockSpec((1, 1, N_MEM, MEM_HEADS, MEM_DH), lambda bi, ti: (layer, bi, 0, 0, 0))
    else:
        mem = pl.BlockSpec((1, 1, N_MEM, d), lambda bi, ti: (layer, bi, 0, 0))
    return pl.pallas_call(
        functools.partial(_memattn_kernel, head_axis=head_axis), name="mem_attn", grid=(b, t // tq),
        in_specs=[tile, mem, mem], out_specs=tile,
        out_shape=jax.ShapeDtypeStruct((b, t, d), F32),
        compiler_params=_params(("parallel", "parallel")))(q, mk, mv)


PEER_DEPTH = PEER_TOPK + 1
PEER_VROWS = 24


def _peer_select_kernel(x_ref, g_ref, wqt_ref, sk_ref, ht_ref, d_ref, e1_ref, s2_ref, e2_ref,
                        qt_scr, v_scr, *, tt):
    h = _rms(x_ref[...], g_ref[...])
    ht = h.T.astype(BF16)
    ht_ref[...] = ht
    qt_scr[...] = _dot(wqt_ref[...], ht)
    half = PEER_TOPK // 2

    def top_values(s, slot):
        for k in range(PEER_DEPTH):
            m = jnp.max(s, axis=0, keepdims=True)
            v_scr[slot, k:k + 1, :] = m
            if k + 1 < PEER_DEPTH:
                s = jnp.where(s == m, NEG_INF, s)

    def head(hh, carry):
        base = pl.multiple_of(hh * 2 * PEER_NKEYS, 2 * PEER_NKEYS)
        q1 = qt_scr[pl.ds(base, PEER_NKEYS), :].astype(BF16)
        q2 = qt_scr[pl.ds(base + PEER_NKEYS, PEER_NKEYS), :].astype(BF16)
        s1 = _dot(sk_ref[2 * hh], q1)
        s2 = _dot(sk_ref[2 * hh + 1], q2)
        top_values(s1, 0)
        top_values(s2, 1)
        v1 = v_scr[0]
        v2 = v_scr[1]
        cands = [v1[a:a + 1, :] + v2[0:half, :] for a in range(half)]
        cands.append(v1[0:1, :] + v2[half:PEER_TOPK, :])
        cands.append(v1[half:PEER_TOPK, :] + v2[0:1, :])
        cands.append(v1[0:1, :] + v2[PEER_TOPK:PEER_VROWS, :])
        cands.append(v1[PEER_TOPK:PEER_VROWS, :] + v2[0:1, :])
        tops = []
        for k in range(PEER_DEPTH):
            m = functools.reduce(jnp.maximum, cands)
            m = jnp.max(m, axis=0, keepdims=True)
            tops.append(m)
            if k + 1 < PEER_DEPTH:
                cands = [jnp.where(c == m, NEG_INF, c) for c in cands]
        z = functools.reduce(lambda a, b: a + b, [jnp.exp(t - tops[0]) for t in tops[:PEER_TOPK]])
        thr = 0.5 * (tops[PEER_TOPK - 1] + tops[PEER_TOPK])
        row = pl.multiple_of(hh * PEER_NKEYS, PEER_NKEYS)
        d_ref[pl.ds(row, PEER_NKEYS), :] = thr - s1
        e1_ref[pl.ds(row, PEER_NKEYS), :] = jnp.exp(s1 - v1[0:1, :]) / z
        s2_ref[pl.ds(row, PEER_NKEYS), :] = s2
        e2_ref[pl.ds(row, PEER_NKEYS), :] = jnp.exp(s2 - v2[0:1, :])
        return carry

    v_scr[:, PEER_TOPK:, :] = jnp.full((2, PEER_VROWS - PEER_TOPK, tt), NEG_INF, F32)
    lax.fori_loop(0, PEER_HEADS, head, 0)


def peer_select(x, gain, wq_t, subkeys, *, tt=512):
    n, d = x.shape
    tt = min(tt, n)
    assert n % tt == 0
    rows = PEER_HEADS * PEER_NKEYS
    col = lambda r: pl.BlockSpec((r, tt), lambda i: (0, i))
    kern = functools.partial(_peer_select_kernel, tt=tt)
    return pl.pallas_call(
        kern, name="peer_select", grid=(n // tt,),
        in_specs=[pl.BlockSpec((tt, d), lambda i: (i, 0)),
                  pl.BlockSpec((1, d), lambda i: (0, 0)),
                  pl.BlockSpec(wq_t.shape, lambda i: (0, 0)),
                  pl.BlockSpec(subkeys.shape, lambda i: (0, 0, 0))],
        out_specs=[col(d), col(rows), col(rows), col(rows), col(rows)],
        out_shape=[jax.ShapeDtypeStruct((d, n), BF16)] + [jax.ShapeDtypeStruct((rows, n), F32)] * 4,
        scratch_shapes=[pltpu.VMEM((2 * rows, tt), F32), pltpu.VMEM((2, PEER_VROWS, tt), F32)],
        compiler_params=_params(("parallel",)))(x, gain.reshape(1, d), wq_t, subkeys)


PEER_CHUNK = 512


def _peer_dense_kernel(x_ref, ht_ref, d_ref, e1_ref, s2_ref, e2_ref, u_ref, vt_ref, o_ref, acc_scr, *, tt):
    c = pl.program_id(1)
    rows_per_chunk = PEER_CHUNK // PEER_NKEYS
    groups = PEER_NKEYS // SUBLANES

    @pl.when(c == 0)
    def _():
        acc_scr[...] = jnp.zeros(acc_scr.shape, F32)

    for t0 in range(0, tt, MXU_WIDTH):
        ls = slice(t0, t0 + MXU_WIDTH)
        a = _dot(u_ref[...], ht_ref[:, ls])
        ps = []
        for r in range(rows_per_chunk):
            i = c * rows_per_chunk + r
            w = None
            for hh in range(PEER_HEADS):
                drow = jnp.broadcast_to(d_ref[pl.ds(hh * PEER_NKEYS + i, 1), ls], (SUBLANES, MXU_WIDTH))
                erow = jnp.broadcast_to(e1_ref[pl.ds(hh * PEER_NKEYS + i, 1), ls], (SUBLANES, MXU_WIDTH))
                s2 = s2_ref[hh * PEER_NKEYS:(hh + 1) * PEER_NKEYS, ls].reshape(groups, SUBLANES, MXU_WIDTH)
                e2 = e2_ref[hh * PEER_NKEYS:(hh + 1) * PEER_NKEYS, ls].reshape(groups, SUBLANES, MXU_WIDTH)
                sel = jnp.where(s2 >= drow[None], erow[None] * e2, 0.0)
                w = sel if w is None else w + sel
            ar = a[r * PEER_NKEYS:(r + 1) * PEER_NKEYS].reshape(groups, SUBLANES, MXU_WIDTH)
            ps.append((_gelu(ar) * w).reshape(PEER_NKEYS, MXU_WIDTH).astype(BF16))
        p = jnp.concatenate(ps, axis=0)
        acc_scr[:, ls] += _dot(vt_ref[...], p)

    @pl.when(c == pl.num_programs(1) - 1)
    def _():
        o_ref[...] = x_ref[...] + acc_scr[...].T


def peer_dense(x, ht, d_thr, e1, s2, e2, u_tab, v_tab_t, *, tt=512):
    n, d = x.shape
    tt = min(tt, n)
    assert n % tt == 0 and tt % MXU_WIDTH == 0
    rows = PEER_HEADS * PEER_NKEYS
    col = lambda r: pl.BlockSpec((r, tt), lambda i, c: (0, i))
    return pl.pallas_call(
        functools.partial(_peer_dense_kernel, tt=tt), name="peer_dense",
        grid=(n // tt, PEER_EXPERTS // PEER_CHUNK),
        in_specs=[pl.BlockSpec((tt, d), lambda i, c: (i, 0)),
                  col(d), col(rows), col(rows), col(rows), col(rows),
                  pl.BlockSpec((PEER_CHUNK, d), lambda i, c: (c, 0)),
                  pl.BlockSpec((d, PEER_CHUNK), lambda i, c: (0, c))],
        out_specs=pl.BlockSpec((tt, d), lambda i, c: (i, 0)),
        out_shape=jax.ShapeDtypeStruct((n, d), F32),
        scratch_shapes=[pltpu.VMEM((d, tt), F32)],
        compiler_params=_params(("parallel", "arbitrary")))(x, ht, d_thr, e1, s2, e2, u_tab, v_tab_t)


def _norm_kernel(x_ref, g_ref, o_ref):
    o_ref[...] = _rms(x_ref[...], g_ref[...])


def final_norm(x, gain, *, tn=512):
    n, d = x.shape
    tn = min(tn, n)
    assert n % tn == 0
    tile = pl.BlockSpec((tn, d), lambda i: (i, 0))
    return pl.pallas_call(
        _norm_kernel, name="final_norm", grid=(n // tn,),
        in_specs=[tile, pl.BlockSpec((1, d), lambda i: (0, 0))], out_specs=tile,
        out_shape=jax.ShapeDtypeStruct((n, d), F32),
        compiler_params=_params(("parallel",)))(x, gain.reshape(1, d))


def _trunk(x, pos0, mem_k, mem_v, pool_past, conv_past, cache_k, cache_v, page_table, W):
    b, t, d = x.shape
    n = b * t
    x = x.reshape(n, d)
    new_k, new_v, new_pool, new_conv = [], [], [], []
    for l in range(DEPTH):
        i = l // 2
        if l % 2 == 0:
            lam_init = 0.8 - 0.6 * math.exp(-0.3 * l)
            q, k, v, p = fused_linear([x], [W["w_in_ab"][i]], name="mix_in_ab", gain=W["norm_mix"][l],
                                      splits=[A_QK] * 4)
            q3, k3, v3, p3 = (a.reshape(b, t, A_QK) for a in (q, k, v, p))
            lam_vecs = [W[nm][i] for nm in ("lambda_q1", "lambda_k1", "lambda_q2", "lambda_k2")]
            if page_table is None:
                o = dattn_prompt(q3, k3, v3, lam_vecs, W["subln_g"][i], lam_init)
            else:
                o = dattn_sample(q3, k3, v3, cache_k, cache_v, i, page_table, lam_vecs,
                                 W["subln_g"][i], lam_init)
            past = pool_past[i]
            past16 = jnp.pad(past, ((0, 0), (1, 0), (0, 0)))
            pooled = pool_mix(p3, past16, pos0, W["pool_w"][i], W["pool_scale"][i])
            x = fused_linear([o.reshape(n, A_QK), pooled.reshape(n, POOL_DIM)],
                             [W["w_out_ab"][i][:A_QK], W["w_out_ab"][i][A_QK:]], name="mix_out_ab", res=x)
            new_k.append(k3.reshape(b, t, A_HEADS, A_HD))
            new_v.append(v3.reshape(b, t, A_HEADS, A_HD))
            new_pool.append(jnp.concatenate([past, p3], axis=1)[:, -(POOL_MAX - 1):])
        else:
            bg, u = fused_linear([x], [W["w_in_c"][i]], name="mix_in_c", gain=W["norm_mix"][l], gate=True)
            u3 = u.reshape(b, t, d)
            past = conv_past[i]
            past8 = jnp.pad(past, ((0, 0), (CONV_HALO - (CONV_W - 1), 0), (0, 0)))
            z = short_conv(u3, bg.reshape(b, t, d), past8, W["conv_w"][i])
            x = fused_linear([z.reshape(n, d)], [W["w_out_c"][i]], name="mix_out_c", res=x)
            new_conv.append(jnp.concatenate([past, u3], axis=1)[:, -(CONV_W - 1):])
        qm = fused_linear([x], [W["w_mq"][l]], name="mem_q", gain=W["norm_mem"][l])
        om = mem_attn(qm.reshape(b, t, d), mem_k, mem_v, l)
        x = fused_linear([om.reshape(n, d)], [W["w_mo"][l]], name="mem_o", res=x)
        sel = peer_select(x, W["norm_ffn"][l], W["peer_wq_t"][l], W["peer_subkeys"][l])
        x = peer_dense(x, *sel, W["peer_u"][l], W["peer_v_t"][l])
    y = final_norm(x, W["norm_final"]).reshape(b, t, d)
    return y, jnp.stack(new_k), jnp.stack(new_v), jnp.stack(new_pool), jnp.stack(new_conv)


def _bf(w):
    return w.astype(BF16)


def _prepare_weights(**w):
    out = dict(w)
    for name in ("w_in_ab", "w_out_ab", "pool_w", "w_in_c", "w_out_c", "w_mq", "w_mo", "peer_u"):
        out[name] = _bf(w[name])
    out["peer_wq_t"] = _bf(jnp.swapaxes(w["peer_wq"], 1, 2))
    out["peer_subkeys"] = _bf(w["peer_subkeys"].reshape(DEPTH, PEER_HEADS * 2, PEER_NKEYS, PEER_NKEYS))
    out["peer_v_t"] = _bf(jnp.swapaxes(w["peer_v"], 1, 2))
    return out


def kernel(x_prompt, x_sample, cache_attn_k, cache_attn_v, state_pool, state_conv, cache_mem_k, cache_mem_v, page_table, mem_prompt, norm_mix, norm_mem, norm_ffn, norm_final, w_in_ab, w_out_ab, lambda_q1, lambda_k1, lambda_q2, lambda_k2, subln_g, pool_w, pool_scale, w_in_c, conv_w, w_out_c, w_mq, w_mk, w_mv, w_mo, peer_wq, peer_subkeys, peer_u, peer_v):
    W = _prepare_weights(
        norm_mix=norm_mix, norm_mem=norm_mem, norm_ffn=norm_ffn, norm_final=norm_final, w_in_ab=w_in_ab,
        w_out_ab=w_out_ab, lambda_q1=lambda_q1, lambda_k1=lambda_k1, lambda_q2=lambda_q2, lambda_k2=lambda_k2,
        subln_g=subln_g, pool_w=pool_w, pool_scale=pool_scale, w_in_c=w_in_c, conv_w=conv_w, w_out_c=w_out_c,
        w_mq=w_mq, w_mo=w_mo, peer_wq=peer_wq, peer_subkeys=peer_subkeys, peer_u=peer_u, peer_v=peer_v)
    bp, sp, d = x_prompt.shape
    mk_p, mv_p = mem_project(mem_prompt.reshape(bp * N_MEM, d), _bf(w_mk), _bf(w_mv))
    mk_p = mk_p.reshape(DEPTH, bp, N_MEM, d)
    mv_p = mv_p.reshape(DEPTH, bp, N_MEM, d)
    pool0 = jnp.zeros((w_in_ab.shape[0], bp, POOL_MAX - 1, POOL_DIM), F32)
    conv0 = jnp.zeros((w_in_c.shape[0], bp, CONV_W - 1, d), F32)
    y_p, k_p, v_p, pool_p, conv_p = _trunk(x_prompt, 0, mk_p, mv_p, pool0, conv0, None, None, None, W)
    past_len = page_table.shape[1] * cache_attn_k.shape[2]
    n_layers, n_pool = cache_attn_k.shape[:2]
    ck = cache_attn_k.reshape(n_layers, n_pool, PAGE_SIZE * A_HEADS, A_HD)
    cv = cache_attn_v.reshape(n_layers, n_pool, PAGE_SIZE * A_HEADS, A_HD)
    y_s, k_s, v_s, pool_s, conv_s = _trunk(
        x_sample, past_len, cache_mem_k, cache_mem_v, state_pool, state_conv, ck, cv, page_table, W)
    mshape = (DEPTH, bp, N_MEM, MEM_HEADS, MEM_DH)
    return (y_p, y_s, k_p, v_p, pool_p, conv_p, mk_p.reshape(mshape), mv_p.reshape(mshape),
            k_s, v_s, pool_s, conv_s)
```

```python
import functools
import math

import jax
import jax.numpy as jnp
from jax import lax
from jax.experimental import pallas as pl
from jax.experimental.pallas import tpu as pltpu

F32 = jnp.float32
BF16 = jnp.bfloat16

D_MODEL = 1024
DEPTH = 4
EPS = 1e-5
A_HEADS = 4
A_DH = 64
A_HD = 2 * A_DH
A_QK = A_HEADS * A_HD
POOL_WINDOWS = (2, 4, 8, 16)
POOL_GDIM = 128
POOL_DIM = 512
POOL_MAX = 16
CONV_W = 3
N_MEM = 256
MEM_HEADS = 4
MEM_DH = 256
PEER_HEADS = 8
PEER_NKEYS = 128
PEER_EXPERTS = PEER_NKEYS * PEER_NKEYS
PEER_TOPK = 16
PAGE_SIZE = 128

SUBLANES = 8
LANES = 128
MXU_WIDTH = 256
VMEM_LIMIT = 56 * 1024 * 1024
NEG_INF = float("-inf")
MASK_VALUE = float(jnp.finfo(jnp.float32).min)
LOG2E = 1.4426950408889634


def _params(sem, vmem=VMEM_LIMIT, flags=None):
    return pltpu.CompilerParams(dimension_semantics=sem, vmem_limit_bytes=vmem, flags=flags)


def _rms(x, g):
    return x * lax.rsqrt(jnp.mean(x * x, axis=-1, keepdims=True) + EPS) * g


def _gelu(a):
    return 0.5 * a * (1.0 + lax.erf(a * (2.0 ** -0.5)))


def _dot(a, b):
    return jnp.dot(a, b, preferred_element_type=F32)


def _dot_nt(a, b):
    return lax.dot_general(a, b, (((1,), (1,)), ((), ())), preferred_element_type=F32)


def _linear_kernel(*refs, n_in, has_norm, has_res, splits, gate):
    a_refs = refs[:n_in]
    w_refs = refs[n_in:2 * n_in]
    idx = 2 * n_in
    g_ref = refs[idx] if has_norm else None
    idx += int(has_norm)
    r_ref = refs[idx] if has_res else None
    idx += int(has_res)
    out_refs = refs[idx:]
    acc = None
    for a_ref, w_ref in zip(a_refs, w_refs):
        a = a_ref[...]
        if has_norm:
            a = _rms(a, g_ref[...])
        d = _dot(a.astype(BF16), w_ref[...])
        acc = d if acc is None else acc + d
    if has_res:
        acc = acc + r_ref[...]
    if gate:
        d3 = acc.shape[1] // 3
        out_refs[0][...] = acc[:, :d3]
        out_refs[1][...] = acc[:, d3:2 * d3] * acc[:, 2 * d3:]
    else:
        off = 0
        for o_ref, s in zip(out_refs, splits):
            o_ref[...] = acc[:, off:off + s]
            off += s


def fused_linear(a_list, w_list, *, name, gain=None, res=None, splits=None, gate=False, tn=512):
    n = a_list[0].shape[0]
    m = w_list[0].shape[1]
    tn = min(tn, n)
    assert n % tn == 0
    if gate:
        out_cols = [m // 3, m // 3]
    else:
        out_cols = list(splits) if splits is not None else [m]
        assert sum(out_cols) == m
    in_specs, args = [], []
    for a in a_list:
        in_specs.append(pl.BlockSpec((tn, a.shape[1]), lambda i: (i, 0)))
        args.append(a)
    for w in w_list:
        in_specs.append(pl.BlockSpec(w.shape, lambda i: (0, 0)))
        args.append(w)
    if gain is not None:
        in_specs.append(pl.BlockSpec((1, gain.shape[-1]), lambda i: (0, 0)))
        args.append(gain.reshape(1, -1))
    if res is not None:
        in_specs.append(pl.BlockSpec((tn, m), lambda i: (i, 0)))
        args.append(res)
    out_shape = [jax.ShapeDtypeStruct((n, c), F32) for c in out_cols]
    out_specs = [pl.BlockSpec((tn, c), lambda i: (i, 0)) for c in out_cols]
    kern = functools.partial(_linear_kernel, n_in=len(a_list), has_norm=gain is not None,
                             has_res=res is not None, splits=tuple(out_cols), gate=gate)
    outs = pl.pallas_call(
        kern, name=name, grid=(n // tn,), in_specs=in_specs, out_specs=out_specs, out_shape=out_shape,
        compiler_params=_params(("parallel",)))(*args)
    return outs if len(outs) > 1 else outs[0]


def _memproj_kernel(a_ref, wk_ref, wv_ref, ok_ref, ov_ref):
    a = a_ref[...].astype(BF16)
    ok_ref[0] = _dot(a, wk_ref[0])
    ov_ref[0] = _dot(a, wv_ref[0])


def mem_project(mem2d, wk, wv):
    n, d = mem2d.shape
    depth = wk.shape[0]
    spec_w = pl.BlockSpec((1, d, d), lambda l: (l, 0, 0))
    spec_o = pl.BlockSpec((1, n, d), lambda l: (l, 0, 0))
    return pl.pallas_call(
        _memproj_kernel, name="mem_project", grid=(depth,),
        in_specs=[pl.BlockSpec((n, d), lambda l: (0, 0)), spec_w, spec_w],
        out_specs=[spec_o, spec_o],
        out_shape=[jax.ShapeDtypeStruct((depth, n, d), F32)] * 2,
        compiler_params=_params(("parallel",)))(mem2d, wk, wv)


def _lambda_value(lq1, lk1, lq2, lk2, lam_init):
    return (jnp.exp(jnp.sum(lq1[...] * lk1[...], axis=-1, keepdims=True))
            - jnp.exp(jnp.sum(lq2[...] * lk2[...], axis=-1, keepdims=True)) + lam_init)


def _subln(o, g, lam_init):
    return _rms(o, g) * (1.0 - lam_init)


def _dattn_prompt_kernel(lq1, lk1, lq2, lk2, g_ref, q_ref, k_ref, v_ref, o_ref, kb_scr, vt_scr, *,
                         tq, lam_init):
    qi = pl.program_id(2)
    nblk = kb_scr.shape[0]

    @pl.when(qi == 0)
    def _():
        for jb in range(nblk):
            kb_scr[jb] = k_ref[0, jb * tq:(jb + 1) * tq, :].astype(BF16)
            vt_scr[jb] = v_ref[0, jb * tq:(jb + 1) * tq, :].T.astype(BF16)

    qt = (q_ref[0] * (A_DH ** -0.5 * LOG2E)).T
    sub = lax.broadcasted_iota(jnp.int32, (A_HD, tq), 0)
    qz = (jnp.where(sub < A_DH, qt, 0.0).astype(BF16), jnp.where(sub >= A_DH, qt, 0.0).astype(BF16))

    def block(j, carry, masked):
        kb = kb_scr[j]
        vt = vt_scr[j]
        out = []
        scores = [_dot(kb, qz[mi]) for mi in range(2)]
        for mi in range(2):
            m, l, a = carry[3 * mi:3 * mi + 3]
            s = scores[mi]
            if masked:
                kk = lax.broadcasted_iota(jnp.int32, (tq, tq), 0)
                qq = lax.broadcasted_iota(jnp.int32, (tq, tq), 1)
                s = jnp.where(kk <= qq, s, MASK_VALUE)
            m_new = jnp.maximum(m, jnp.max(s, axis=0, keepdims=True))
            alpha = jnp.exp2(m - m_new)
            p = jnp.exp2(s - m_new)
            l = alpha * l + jnp.sum(p, axis=0, keepdims=True)
            a = alpha * a + _dot(vt, p.astype(BF16))
            out += [m_new, l, a]
        return tuple(out)

    init = (jnp.full((1, tq), NEG_INF, F32), jnp.zeros((1, tq), F32), jnp.zeros((A_HD, tq), F32)) * 2
    carry = lax.fori_loop(0, qi, lambda j, c: block(j, c, False), init)
    m1, l1, a1, m2, l2, a2 = block(qi, carry, True)
    lam = _lambda_value(lq1, lk1, lq2, lk2, lam_init)
    ot = a1 / l1 - lam * (a2 / l2)
    o_ref[0] = _subln(ot.T, g_ref[...], lam_init)


def dattn_prompt(q, k, v, lam_vecs, subln_g, lam_init, *, tq=512):
    b, t, _ = q.shape
    tq = min(tq, t)
    assert t % tq == 0
    vec = pl.BlockSpec((1, A_DH), lambda bi, h, qi: (0, 0))
    kern = functools.partial(_dattn_prompt_kernel, tq=tq, lam_init=lam_init)
    return pl.pallas_call(
        kern, name="dattn_prompt", grid=(b, A_HEADS, t // tq),
        in_specs=[vec, vec, vec, vec,
                  pl.BlockSpec((1, A_HD), lambda bi, h, qi: (0, 0)),
                  pl.BlockSpec((1, tq, A_HD), lambda bi, h, qi: (bi, qi, h)),
                  pl.BlockSpec((1, t, A_HD), lambda bi, h, qi: (bi, 0, h)),
                  pl.BlockSpec((1, t, A_HD), lambda bi, h, qi: (bi, 0, h))],
        out_specs=pl.BlockSpec((1, tq, A_HD), lambda bi, h, qi: (bi, qi, h)),
        out_shape=jax.ShapeDtypeStruct((b, t, A_QK), F32),
        scratch_shapes=[pltpu.VMEM((t // tq, tq, A_HD), BF16), pltpu.VMEM((t // tq, A_HD, tq), BF16)],
        compiler_params=_params(("parallel", "parallel", "arbitrary")))(
            *[x.reshape(1, A_DH) for x in lam_vecs], subln_g.reshape(1, A_HD), q, k, v)


def _dattn_sample_kernel(pt_ref, lq1, lk1, lq2, lk2, g_ref, q_ref, kn_ref, vn_ref, *rest,
                         n_pages, t_new, lam_init):
    k_refs = rest[:n_pages]
    v_refs = rest[n_pages:2 * n_pages]
    o_ref = rest[2 * n_pages]
    rows = A_HEADS * 2 * t_new
    hbits = A_HEADS.bit_length() - 1
    tbits = t_new.bit_length() - 1
    q = q_ref[0] * (A_DH ** -0.5)
    lane = lax.broadcasted_iota(jnp.int32, (t_new, A_HD), 1)
    parts = []
    for h in range(A_HEADS):
        qh = q[:, h * A_HD:(h + 1) * A_HD]
        parts += [jnp.where(lane < A_DH, qh, 0.0), jnp.where(lane >= A_DH, qh, 0.0)]
    qz = jnp.concatenate(parts, axis=0).astype(BF16)

    def bias(ncols, causal):
        rr = lax.broadcasted_iota(jnp.int32, (rows, ncols), 0)
        cc = lax.broadcasted_iota(jnp.int32, (rows, ncols), 1)
        ok = (cc & (A_HEADS - 1)) == (rr >> (tbits + 1))
        if causal:
            ok = ok & ((cc >> hbits) <= (rr & (t_new - 1)))
        return jnp.where(ok, 0.0, NEG_INF)

    page_bias = bias(PAGE_SIZE * A_HEADS, False)
    s_pages = [_dot_nt(qz, k_ref[0, 0].astype(BF16)) + page_bias for k_ref in k_refs]
    s_new = _dot_nt(qz, kn_ref[0].astype(BF16)) + bias(t_new * A_HEADS, True)
    m = jnp.max(functools.reduce(jnp.maximum, s_pages), axis=-1, keepdims=True)
    m = jnp.maximum(m, jnp.max(s_new, axis=-1, keepdims=True))
    p_new = jnp.exp(s_new - m)
    l = jnp.sum(p_new, axis=-1, keepdims=True)
    acc = _dot(p_new.astype(BF16), vn_ref[0].astype(BF16))
    for s, v_ref in zip(s_pages, v_refs):
        p = jnp.exp(s - m)
        l = l + jnp.sum(p, axis=-1, keepdims=True)
        acc = acc + _dot(p.astype(BF16), v_ref[0, 0].astype(BF16))
    acc = acc / l
    lam = _lambda_value(lq1, lk1, lq2, lk2, lam_init)
    for h in range(A_HEADS):
        r0 = h * 2 * t_new
        o = acc[r0:r0 + t_new] - lam * acc[r0 + t_new:r0 + 2 * t_new]
        o_ref[0, :, h * A_HD:(h + 1) * A_HD] = _subln(o, g_ref[...], lam_init)


def dattn_sample(q, k_new, v_new, cache_k, cache_v, layer, page_table, lam_vecs, subln_g, lam_init):
    b, t_new, _ = q.shape
    assert t_new & (t_new - 1) == 0
    n_pages = page_table.shape[1]
    pt = page_table.reshape(-1)
    vec = pl.BlockSpec((1, A_DH), lambda bi, pt: (0, 0))
    qspec = pl.BlockSpec((1, t_new, A_QK), lambda bi, pt: (bi, 0, 0))
    nspec = pl.BlockSpec((1, t_new * A_HEADS, A_HD), lambda bi, pt: (bi, 0, 0))

    def page(r):
        return pl.BlockSpec((1, 1, PAGE_SIZE * A_HEADS, A_HD),
                            lambda bi, pt: (layer, pt[bi * n_pages + r], 0, 0))

    pages = [page(r) for r in range(n_pages)]
    kern = functools.partial(_dattn_sample_kernel, n_pages=n_pages, t_new=t_new, lam_init=lam_init)
    grid_spec = pltpu.PrefetchScalarGridSpec(
        num_scalar_prefetch=1, grid=(b,),
        in_specs=[vec, vec, vec, vec, pl.BlockSpec((1, A_HD), lambda bi, pt: (0, 0)),
                  qspec, nspec, nspec] + pages + pages,
        out_specs=qspec)
    return pl.pallas_call(
        kern, name="dattn_sample", grid_spec=grid_spec,
        out_shape=jax.ShapeDtypeStruct((b, t_new, A_QK), F32),
        compiler_params=_params(("parallel",)))(
            pt, *[x.reshape(1, A_DH) for x in lam_vecs], subln_g.reshape(1, A_HD), q,
            k_new.reshape(b, t_new * A_HEADS, A_HD), v_new.reshape(b, t_new * A_HEADS, A_HD),
            *([cache_k] * n_pages), *([cache_v] * n_pages))


def _pool_kernel(p_ref, halo_ref, past_ref, w_ref, sc_ref, o_ref, scr, *, tt, pos0):
    ti = pl.program_id(1)
    scr[0:POOL_MAX, :] = jnp.where(ti == 0, past_ref[0], halo_ref[0])
    scr[POOL_MAX:, :] = p_ref[0]
    pos = pos0 + ti * tt + lax.broadcasted_iota(jnp.int32, (tt, POOL_GDIM), 0)
    for g, w in enumerate(POOL_WINDOWS):
        sl = slice(g * POOL_GDIM, (g + 1) * POOL_GDIM)
        x = scr[POOL_MAX:, sl]
        s = x
        for jj in range(1, w):
            s = s + scr[POOL_MAX - jj:POOL_MAX - jj + tt, sl]
        cnt = jnp.minimum(w, pos + 1).astype(F32)
        d = s / cnt - x
        y = _dot(d.astype(BF16), w_ref[g])
        o_ref[0, :, sl] = y * sc_ref[:, sl]


def pool_mix(p, past16, pos0, pool_w, pool_scale, *, tt=1024):
    b, t, _ = p.shape
    tt = min(tt, t)
    halo_src = p if t >= POOL_MAX else past16
    hb = tt // POOL_MAX
    kern = functools.partial(_pool_kernel, tt=tt, pos0=pos0)
    return pl.pallas_call(
        kern, name="pool_mix", grid=(b, t // tt),
        in_specs=[pl.BlockSpec((1, tt, POOL_DIM), lambda bi, ti: (bi, ti, 0)),
                  pl.BlockSpec((1, POOL_MAX, POOL_DIM), lambda bi, ti: (bi, jnp.maximum(ti * hb - 1, 0), 0)),
                  pl.BlockSpec((1, POOL_MAX, POOL_DIM), lambda bi, ti: (bi, 0, 0)),
                  pl.BlockSpec(pool_w.shape, lambda bi, ti: (0, 0, 0)),
                  pl.BlockSpec((1, POOL_DIM), lambda bi, ti: (0, 0))],
        out_specs=pl.BlockSpec((1, tt, POOL_DIM), lambda bi, ti: (bi, ti, 0)),
        out_shape=jax.ShapeDtypeStruct((b, t, POOL_DIM), F32),
        scratch_shapes=[pltpu.VMEM((POOL_MAX + tt, POOL_DIM), F32)],
        compiler_params=_params(("parallel", "arbitrary")))(
            p, halo_src, past16, pool_w, pool_scale.reshape(1, POOL_DIM))


CONV_HALO = 8


def _conv_kernel(u_ref, halo_ref, past_ref, b_ref, cw_ref, o_ref, scr, *, tt):
    ti = pl.program_id(1)
    scr[0:CONV_HALO, :] = jnp.where(ti == 0, past_ref[0], halo_ref[0])
    scr[CONV_HALO:, :] = u_ref[0]
    y = None
    for jj in range(CONV_W):
        off = CONV_HALO - (CONV_W - 1) + jj
        term = cw_ref[jj:jj + 1, :] * scr[off:off + tt, :]
        y = term if y is None else y + term
    o_ref[0] = b_ref[0] * y


def short_conv(u, bgate, past8, conv_w, *, tt=1024):
    b, t, d = u.shape
    tt = min(tt, t)
    hb = tt // CONV_HALO
    tile = pl.BlockSpec((1, tt, d), lambda bi, ti: (bi, ti, 0))
    kern = functools.partial(_conv_kernel, tt=tt)
    return pl.pallas_call(
        kern, name="short_conv", grid=(b, t // tt),
        in_specs=[tile,
                  pl.BlockSpec((1, CONV_HALO, d), lambda bi, ti: (bi, jnp.maximum(ti * hb - 1, 0), 0)),
                  pl.BlockSpec((1, CONV_HALO, d), lambda bi, ti: (bi, 0, 0)),
                  tile,
                  pl.BlockSpec((CONV_W, d), lambda bi, ti: (0, 0))],
        out_specs=tile,
        out_shape=jax.ShapeDtypeStruct((b, t, d), F32),
        scratch_shapes=[pltpu.VMEM((CONV_HALO + tt, d), F32)],
        compiler_params=_params(("parallel", "arbitrary")))(u, u, past8, bgate, conv_w)


def _memattn_kernel(q_ref, k_ref, v_ref, o_ref):
    scale = MEM_DH ** -0.5
    for h in range(MEM_HEADS):
        sl = slice(h * MEM_DH, (h + 1) * MEM_DH)
        qh = (q_ref[0, :, sl] * scale).astype(BF16)
        kh = k_ref[0, 0, :, sl].astype(BF16)
        vh = v_ref[0, 0, :, sl].astype(BF16)
        s = _dot_nt(qh, kh)
        s = s - jnp.max(s, axis=-1, keepdims=True)
        p = jnp.exp(s)
        p = p / jnp.sum(p, axis=-1, keepdims=True)
        o_ref[0, :, sl] = _dot(p.astype(BF16), vh)


def mem_attn(q, mk, mv, layer, *, tq=512):
    b, t, d = q.shape
    tq = min(tq, t)
    tile = pl.BlockSpec((1, tq, d), lambda bi, ti: (bi, ti, 0))
    mem = pl.BlockSpec((1, 1, N_MEM, d), lambda bi, ti: (layer, bi, 0, 0))
    return pl.pallas_call(
        _memattn_kernel, name="mem_attn", grid=(b, t // tq),
        in_specs=[tile, mem, mem], out_specs=tile,
        out_shape=jax.ShapeDtypeStruct((b, t, d), F32),
        compiler_params=_params(("parallel", "parallel")))(q, mk, mv)


MEM_SPLIT = MEM_DH // LANES
MEM_ROWS = MEM_HEADS * MEM_SPLIT


def _memattn_rows_kernel(q_ref, k_ref, v_ref, o_ref, *, t):
    ncol = N_MEM * MEM_ROWS
    q = q_ref[0] * (MEM_DH ** -0.5)
    qz = jnp.concatenate([q[:, kk * LANES:(kk + 1) * LANES] for kk in range(MEM_ROWS)], axis=0)
    sfull = _dot_nt(qz.astype(BF16), k_ref[0, 0].astype(BF16))
    sfull = sfull.reshape(MEM_HEADS, MEM_SPLIT, t, ncol)
    s = sfull[:, 0]
    for j in range(1, MEM_SPLIT):
        part = sfull[:, j].reshape(MEM_HEADS * t, ncol)
        s = s + pltpu.roll(part, ncol - j * MEM_HEADS, axis=1).reshape(MEM_HEADS, t, ncol)
    s = s.reshape(MEM_HEADS * t, ncol)
    rr = lax.broadcasted_iota(jnp.int32, (MEM_HEADS * t, ncol), 0)
    cc = lax.broadcasted_iota(jnp.int32, (MEM_HEADS * t, ncol), 1)
    valid = (cc & (MEM_ROWS - 1)) == (rr >> (t.bit_length() - 1))
    s = jnp.where(valid, s, NEG_INF)
    s = s - jnp.max(s, axis=-1, keepdims=True)
    p = jnp.exp(s)
    p = p / jnp.sum(p, axis=-1, keepdims=True)
    parts = [p.reshape(MEM_HEADS, t, ncol)]
    for j in range(1, MEM_SPLIT):
        parts.append(pltpu.roll(p, j * MEM_HEADS, axis=1).reshape(MEM_HEADS, t, ncol))
    pz = jnp.stack(parts, axis=1).reshape(MEM_ROWS * t, ncol)
    o = _dot(pz.astype(BF16), v_ref[0, 0].astype(BF16))
    for kk in range(MEM_ROWS):
        o_ref[0, :, kk * LANES:(kk + 1) * LANES] = o[kk * t:(kk + 1) * t]


def mem_attn_rows(q, mk_rows, mv_rows, layer):
    b, t, d = q.shape
    assert t & (t - 1) == 0 and t % SUBLANES == 0
    tile = pl.BlockSpec((1, t, d), lambda bi: (bi, 0, 0))
    mem = pl.BlockSpec((1, 1, N_MEM * MEM_ROWS, LANES), lambda bi: (layer, bi, 0, 0))
    return pl.pallas_call(
        functools.partial(_memattn_rows_kernel, t=t), name="mem_attn_rows", grid=(b,),
        in_specs=[tile, mem, mem], out_specs=tile,
        out_shape=jax.ShapeDtypeStruct((b, t, d), F32),
        compiler_params=_params(("parallel",)))(q, mk_rows, mv_rows)


def _mem_rows_view(cache):
    depth, b = cache.shape[:2]
    c = cache.reshape(depth, b, N_MEM, MEM_HEADS, MEM_SPLIT, LANES)
    return jnp.swapaxes(c, 3, 4).reshape(depth, b, N_MEM * MEM_ROWS, LANES)


PEER_DEPTH = PEER_TOPK + 1
PEER_VROWS = 24


def _peer_select_kernel(x_ref, g_ref, wqt_ref, sk_ref, ht_ref, d_ref, e1_ref, s2_ref, e2_ref,
                        qt_scr, v_scr, *, tt):
    h = _rms(x_ref[...], g_ref[...])
    ht = h.T.astype(BF16)
    ht_ref[...] = ht
    qt_scr[...] = _dot(wqt_ref[...], ht)
    half = PEER_TOPK // 2

    def value_step(k, ss, ls, vb):
        ms = [jnp.max(s, axis=0, keepdims=True) for s in ss]
        for slot in range(2):
            v_scr[vb + slot, k:k + 1, ls] = ms[slot]
        if k + 1 < PEER_DEPTH:
            ss = [jnp.where(s == m, NEG_INF, s) for s, m in zip(ss, ms)]
        return ss

    def candidates(ls, vb):
        v1 = v_scr[vb, :, ls]
        v2 = v_scr[vb + 1, :, ls]
        cands = [v1[a:a + 1, :] + v2[0:half, :] for a in range(half)]
        cands.append(v1[0:1, :] + v2[half:PEER_TOPK, :])
        cands.append(v1[half:PEER_TOPK, :] + v2[0:1, :])
        cands.append(v1[0:1, :] + v2[PEER_TOPK:PEER_VROWS, :])
        cands.append(v1[PEER_TOPK:PEER_VROWS, :] + v2[0:1, :])
        return cands, v1[0:1, :], v2[0:1, :]

    def cand_step(k, cands, tops):
        m = functools.reduce(jnp.maximum, cands)
        m = jnp.max(m, axis=0, keepdims=True)
        tops.append(m)
        if k + 1 < PEER_DEPTH:
            cands = [jnp.where(c == m, NEG_INF, c) for c in cands]
        return cands

    def finish(hh, tops, max1, max2, s1, s2, ls):
        z = functools.reduce(lambda a, b: a + b, [jnp.exp(t - tops[0]) for t in tops[:PEER_TOPK]])
        thr = 0.5 * (tops[PEER_TOPK - 1] + tops[PEER_TOPK])
        row = pl.multiple_of(hh * PEER_NKEYS, PEER_NKEYS)
        d_ref[pl.ds(row, PEER_NKEYS), ls] = thr - s1
        e1_ref[pl.ds(row, PEER_NKEYS), ls] = jnp.exp(s1 - max1) / z
        s2_ref[pl.ds(row, PEER_NKEYS), ls] = s2
        e2_ref[pl.ds(row, PEER_NKEYS), ls] = jnp.exp(s2 - max2)

    def head_pair(it, carry):
        head(2 * it, 0)
        head(2 * it + 1, 2)
        return carry

    def head(hh, vb):
        base = pl.multiple_of(hh * 2 * PEER_NKEYS, 2 * PEER_NKEYS)
        q1 = qt_scr[pl.ds(base, PEER_NKEYS), :].astype(BF16)
        q2 = qt_scr[pl.ds(base + PEER_NKEYS, PEER_NKEYS), :].astype(BF16)
        s1 = _dot(sk_ref[2 * hh], q1)
        s2 = _dot(sk_ref[2 * hh + 1], q2)
        lanes = [slice(t0, t0 + LANES) for t0 in range(0, tt, LANES)]
        prev = None
        for ls in lanes + [None]:
            ss = [s1[:, ls], s2[:, ls]] if ls is not None else None
            if prev is not None:
                cands, max1, max2 = candidates(prev, vb)
                tops = []
            for k in range(PEER_DEPTH):
                if ss is not None:
                    ss = value_step(k, ss, ls, vb)
                if prev is not None:
                    cands = cand_step(k, cands, tops)
            if prev is not None:
                finish(hh, tops, max1, max2, s1[:, prev], s2[:, prev], prev)
            prev = ls

    v_scr[:, PEER_TOPK:, :] = jnp.full((4, PEER_VROWS - PEER_TOPK, tt), NEG_INF, F32)
    lax.fori_loop(0, PEER_HEADS // 2, head_pair, 0)


def peer_select(x, gain, wq_t, subkeys, *, tt=512):
    n, d = x.shape
    tt = min(tt, n)
    assert n % tt == 0
    rows = PEER_HEADS * PEER_NKEYS
    col = lambda r: pl.BlockSpec((r, tt), lambda i: (0, i))
    kern = functools.partial(_peer_select_kernel, tt=tt)
    return pl.pallas_call(
        kern, name="peer_select", grid=(n // tt,),
        in_specs=[pl.BlockSpec((tt, d), lambda i: (i, 0)),
                  pl.BlockSpec((1, d), lambda i: (0, 0)),
                  pl.BlockSpec(wq_t.shape, lambda i: (0, 0)),
                  pl.BlockSpec(subkeys.shape, lambda i: (0, 0, 0))],
        out_specs=[col(d), col(rows), col(rows), col(rows), col(rows)],
        out_shape=[jax.ShapeDtypeStruct((d, n), BF16)] + [jax.ShapeDtypeStruct((rows, n), F32)] * 4,
        scratch_shapes=[pltpu.VMEM((2 * rows, tt), F32), pltpu.VMEM((4, PEER_VROWS, tt), F32)],
        compiler_params=_params(("parallel",)))(x, gain.reshape(1, d), wq_t, subkeys)


PEER_CHUNK = 512
PEER_DENSE_FLAGS = None


def _peer_dense_kernel(x_ref, ht_ref, d_ref, e1_ref, s2_ref, e2_ref, u0_ref, un_ref, vtp_ref, vtl_ref, o_ref,
                       acc_scr, a0_scr, a1_scr, p0_scr, p1_scr, *, tt):
    a_scr = (a0_scr, a1_scr)
    p_scr = (p0_scr, p1_scr)
    c = pl.program_id(1)
    last = pl.num_programs(1) - 1
    rows_per_chunk = PEER_CHUNK // PEER_NKEYS
    groups = PEER_NKEYS // SUBLANES

    @pl.when(c == 0)
    def _():
        acc_scr[...] = jnp.zeros(acc_scr.shape, F32)
        p_scr[1][...] = jnp.zeros(p_scr[1].shape, BF16)
        a_scr[0][...] = _dot(u0_ref[...], ht_ref[...])

    def key_rows(r, ls):
        i = c * rows_per_chunk + r
        d_rows = [jnp.broadcast_to(d_ref[pl.ds(hh * PEER_NKEYS + i, 1), ls], (SUBLANES, MXU_WIDTH))
                  for hh in range(PEER_HEADS)]
        e_rows = [jnp.broadcast_to(e1_ref[pl.ds(hh * PEER_NKEYS + i, 1), ls], (SUBLANES, MXU_WIDTH))
                  for hh in range(PEER_HEADS)]
        return d_rows, e_rows

    def weights_item(r, g2, ls, cur, rows):
        d_rows, e_rows = rows
        vals = []
        for g in (g2, g2 + 1):
            w = None
            for hh in range(PEER_HEADS):
                js = slice(hh * PEER_NKEYS + g * SUBLANES, hh * PEER_NKEYS + (g + 1) * SUBLANES)
                sel = jnp.where(s2_ref[js, ls] >= d_rows[hh], e_rows[hh] * e2_ref[js, ls], 0.0)
                w = sel if w is None else w + sel
            a = a_scr[cur][r * PEER_NKEYS + g * SUBLANES:r * PEER_NKEYS + (g + 1) * SUBLANES, ls]
            vals.append(_gelu(a) * w)
        lo = r * PEER_NKEYS + g2 * SUBLANES
        p_scr[cur][lo:lo + 2 * SUBLANES, ls] = jnp.concatenate(vals, axis=0).astype(BF16)

    def preact_piece(k, ls, nxt):
        ks = slice(k * MXU_WIDTH, (k + 1) * MXU_WIDTH)
        part = _dot(un_ref[:, ks], ht_ref[ks, ls])
        if k == 0:
            a_scr[nxt][:, ls] = part
        else:
            a_scr[nxt][:, ls] += part

    def value_piece(k, ls, nxt):
        ks = slice(k * MXU_WIDTH, (k + 1) * MXU_WIDTH)
        acc_scr[:, ls] += _dot(vtp_ref[:, ks], p_scr[nxt][ks, ls])

    def step(cur, nxt):
        pieces = {0: [(preact_piece, 0), (value_piece, 0)], 1: [(preact_piece, 1)],
                  2: [(preact_piece, 2), (value_piece, 1)], 3: [(preact_piece, 3)]}
        for t0 in range(0, tt, MXU_WIDTH):
            ls = slice(t0, t0 + MXU_WIDTH)
            for r in range(rows_per_chunk):
                rows = key_rows(r, ls)
                for g2 in range(0, groups, 2):
                    weights_item(r, g2, ls, cur, rows)
                for fn, k in pieces[r]:
                    fn(k, ls, nxt)

    @pl.when((c & 1) == 0)
    def _():
        step(0, 1)

    @pl.when((c & 1) == 1)
    def _():
        step(1, 0)

    @pl.when(c == last)
    def _():
        acc = acc_scr[...] + _dot(vtl_ref[...], p_scr[1][...])
        o_ref[...] = x_ref[...] + acc.T


def peer_dense(x, ht, d_thr, e1, s2, e2, u_tab, v_tab_t, *, tt=512):
    n, d = x.shape
    tt = min(tt, n)
    assert n % tt == 0 and tt % MXU_WIDTH == 0
    rows = PEER_HEADS * PEER_NKEYS
    nchunks = PEER_EXPERTS // PEER_CHUNK
    assert nchunks % 2 == 0
    col = lambda r: pl.BlockSpec((r, tt), lambda i, c: (0, i))
    return pl.pallas_call(
        functools.partial(_peer_dense_kernel, tt=tt), name="peer_dense",
        grid=(n // tt, nchunks),
        in_specs=[pl.BlockSpec((tt, d), lambda i, c: (i, 0)),
                  col(d), col(rows), col(rows), col(rows), col(rows),
                  pl.BlockSpec((PEER_CHUNK, d), lambda i, c: (0, 0)),
                  pl.BlockSpec((PEER_CHUNK, d), lambda i, c: (jnp.minimum(c + 1, nchunks - 1), 0)),
                  pl.BlockSpec((d, PEER_CHUNK), lambda i, c: (0, jnp.maximum(c - 1, 0))),
                  pl.BlockSpec((d, PEER_CHUNK), lambda i, c: (0, nchunks - 1))],
        out_specs=pl.BlockSpec((tt, d), lambda i, c: (i, 0)),
        out_shape=jax.ShapeDtypeStruct((n, d), F32),
        scratch_shapes=[pltpu.VMEM((d, tt), F32)] + [pltpu.VMEM((PEER_CHUNK, tt), F32)] * 2
        + [pltpu.VMEM((PEER_CHUNK, tt), BF16)] * 2,
        compiler_params=_params(("parallel", "arbitrary"), flags=PEER_DENSE_FLAGS))(
            x, ht, d_thr, e1, s2, e2, u_tab, u_tab, v_tab_t, v_tab_t)


def _norm_kernel(x_ref, g_ref, o_ref):
    o_ref[...] = _rms(x_ref[...], g_ref[...])


def final_norm(x, gain, *, tn=512):
    n, d = x.shape
    tn = min(tn, n)
    assert n % tn == 0
    tile = pl.BlockSpec((tn, d), lambda i: (i, 0))
    return pl.pallas_call(
        _norm_kernel, name="final_norm", grid=(n // tn,),
        in_specs=[tile, pl.BlockSpec((1, d), lambda i: (0, 0))], out_specs=tile,
        out_shape=jax.ShapeDtypeStruct((n, d), F32),
        compiler_params=_params(("parallel",)))(x, gain.reshape(1, d))


def _trunk(x, pos0, mem_k, mem_v, pool_past, conv_past, cache_k, cache_v, page_table, W):
    b, t, d = x.shape
    n = b * t
    x = x.reshape(n, d)
    new_k, new_v, new_pool, new_conv = [], [], [], []
    for l in range(DEPTH):
        i = l // 2
        if l % 2 == 0:
            lam_init = 0.8 - 0.6 * math.exp(-0.3 * l)
            q, k, v, p = fused_linear([x], [W["w_in_ab"][i]], name="mix_in_ab", gain=W["norm_mix"][l],
                                      splits=[A_QK] * 4)
            q3, k3, v3, p3 = (a.reshape(b, t, A_QK) for a in (q, k, v, p))
            lam_vecs = [W[nm][i] for nm in ("lambda_q1", "lambda_k1", "lambda_q2", "lambda_k2")]
            if page_table is None:
                o = dattn_prompt(q3, k3, v3, lam_vecs, W["subln_g"][i], lam_init)
            else:
                o = dattn_sample(q3, k3, v3, cache_k, cache_v, i, page_table, lam_vecs,
                                 W["subln_g"][i], lam_init)
            past = pool_past[i]
            past16 = jnp.pad(past, ((0, 0), (1, 0), (0, 0)))
            pooled = pool_mix(p3, past16, pos0, W["pool_w"][i], W["pool_scale"][i])
            x = fused_linear([o.reshape(n, A_QK), pooled.reshape(n, POOL_DIM)],
                             [W["w_out_ab"][i][:A_QK], W["w_out_ab"][i][A_QK:]], name="mix_out_ab", res=x)
            new_k.append(k3.reshape(b, t, A_HEADS, A_HD))
            new_v.append(v3.reshape(b, t, A_HEADS, A_HD))
            new_pool.append(jnp.concatenate([past, p3], axis=1)[:, -(POOL_MAX - 1):])
        else:
            bg, u = fused_linear([x], [W["w_in_c"][i]], name="mix_in_c", gain=W["norm_mix"][l], gate=True)
            u3 = u.reshape(b, t, d)
            past = conv_past[i]
            past8 = jnp.pad(past, ((0, 0), (CONV_HALO - (CONV_W - 1), 0), (0, 0)))
            z = short_conv(u3, bg.reshape(b, t, d), past8, W["conv_w"][i])
            x = fused_linear([z.reshape(n, d)], [W["w_out_c"][i]], name="mix_out_c", res=x)
            new_conv.append(jnp.concatenate([past, u3], axis=1)[:, -(CONV_W - 1):])
        qm = fused_linear([x], [W["w_mq"][l]], name="mem_q", gain=W["norm_mem"][l])
        if page_table is None:
            om = mem_attn(qm.reshape(b, t, d), mem_k, mem_v, l)
        else:
            om = mem_attn_rows(qm.reshape(b, t, d), mem_k, mem_v, l)
        x = fused_linear([om.reshape(n, d)], [W["w_mo"][l]], name="mem_o", res=x)
        sel = peer_select(x, W["norm_ffn"][l], W["peer_wq_t"][l], W["peer_subkeys"][l])
        x = peer_dense(x, *sel, W["peer_u"][l], W["peer_v_t"][l])
    y = final_norm(x, W["norm_final"]).reshape(b, t, d)
    return y, jnp.stack(new_k), jnp.stack(new_v), jnp.stack(new_pool), jnp.stack(new_conv)


def _bf(w):
    return w.astype(BF16)


def _prepare_weights(**w):
    out = dict(w)
    for name in ("w_in_ab", "w_out_ab", "pool_w", "w_in_c", "w_out_c", "w_mq", "w_mo", "peer_u"):
        out[name] = _bf(w[name])
    out["peer_wq_t"] = _bf(jnp.swapaxes(w["peer_wq"], 1, 2))
    out["peer_subkeys"] = _bf(w["peer_subkeys"].reshape(DEPTH, PEER_HEADS * 2, PEER_NKEYS, PEER_NKEYS))
    out["peer_v_t"] = _bf(jnp.swapaxes(w["peer_v"], 1, 2))
    return out


def kernel(x_prompt, x_sample, cache_attn_k, cache_attn_v, state_pool, state_conv, cache_mem_k, cache_mem_v, page_table, mem_prompt, norm_mix, norm_mem, norm_ffn, norm_final, w_in_ab, w_out_ab, lambda_q1, lambda_k1, lambda_q2, lambda_k2, subln_g, pool_w, pool_scale, w_in_c, conv_w, w_out_c, w_mq, w_mk, w_mv, w_mo, peer_wq, peer_subkeys, peer_u, peer_v):
    W = _prepare_weights(
        norm_mix=norm_mix, norm_mem=norm_mem, norm_ffn=norm_ffn, norm_final=norm_final, w_in_ab=w_in_ab,
        w_out_ab=w_out_ab, lambda_q1=lambda_q1, lambda_k1=lambda_k1, lambda_q2=lambda_q2, lambda_k2=lambda_k2,
        subln_g=subln_g, pool_w=pool_w, pool_scale=pool_scale, w_in_c=w_in_c, conv_w=conv_w, w_out_c=w_out_c,
        w_mq=w_mq, w_mo=w_mo, peer_wq=peer_wq, peer_subkeys=peer_subkeys, peer_u=peer_u, peer_v=peer_v)
    bp, sp, d = x_prompt.shape
    mk_p, mv_p = mem_project(mem_prompt.reshape(bp * N_MEM, d), _bf(w_mk), _bf(w_mv))
    mk_p = mk_p.reshape(DEPTH, bp, N_MEM, d)
    mv_p = mv_p.reshape(DEPTH, bp, N_MEM, d)
    pool0 = jnp.zeros((w_in_ab.shape[0], bp, POOL_MAX - 1, POOL_DIM), F32)
    conv0 = jnp.zeros((w_in_c.shape[0], bp, CONV_W - 1, d), F32)
    y_p, k_p, v_p, pool_p, conv_p = _trunk(x_prompt, 0, mk_p, mv_p, pool0, conv0, None, None, None, W)
    past_len = page_table.shape[1] * cache_attn_k.shape[2]
    n_layers, n_pool = cache_attn_k.shape[:2]
    ck = cache_attn_k.reshape(n_layers, n_pool, PAGE_SIZE * A_HEADS, A_HD)
    cv = cache_attn_v.reshape(n_layers, n_pool, PAGE_SIZE * A_HEADS, A_HD)
    y_s, k_s, v_s, pool_s, conv_s = _trunk(
        x_sample, past_len, _mem_rows_view(cache_mem_k), _mem_rows_view(cache_mem_v), state_pool, state_conv,
        ck, cv, page_table, W)
    mshape = (DEPTH, bp, N_MEM, MEM_HEADS, MEM_DH)
    return (y_p, y_s, k_p, v_p, pool_p, conv_p, mk_p.reshape(mshape), mv_p.reshape(mshape),
            k_s, v_s, pool_s, conv_s)
```

```python
import functools
import math

import jax
import jax.numpy as jnp
from jax import lax
from jax.experimental import pallas as pl
from jax.experimental.pallas import tpu as pltpu

F32 = jnp.float32
BF16 = jnp.bfloat16

D_MODEL = 1024
DEPTH = 4
EPS = 1e-5
A_HEADS = 4
A_DH = 64
A_HD = 2 * A_DH
A_QK = A_HEADS * A_HD
POOL_WINDOWS = (2, 4, 8, 16)
POOL_GDIM = 128
POOL_DIM = 512
POOL_MAX = 16
CONV_W = 3
N_MEM = 256
MEM_HEADS = 4
MEM_DH = 256
PEER_HEADS = 8
PEER_NKEYS = 128
PEER_EXPERTS = PEER_NKEYS * PEER_NKEYS
PEER_TOPK = 16
PAGE_SIZE = 128

SUBLANES = 8
LANES = 128
MXU_WIDTH = 256
VMEM_LIMIT = 56 * 1024 * 1024
NEG_INF = float("-inf")
MASK_VALUE = float(jnp.finfo(jnp.float32).min)
LOG2E = 1.4426950408889634


def _params(sem, vmem=VMEM_LIMIT, flags=None):
    return pltpu.CompilerParams(dimension_semantics=sem, vmem_limit_bytes=vmem, flags=flags)


def _rms(x, g):
    return x * lax.rsqrt(jnp.mean(x * x, axis=-1, keepdims=True) + EPS) * g


def _gelu(a):
    return 0.5 * a * (1.0 + lax.erf(a * (2.0 ** -0.5)))


def _dot(a, b):
    return jnp.dot(a, b, preferred_element_type=F32)


def _dot_nt(a, b):
    return lax.dot_general(a, b, (((1,), (1,)), ((), ())), preferred_element_type=F32)


def _linear_kernel(*refs, n_in, has_norm, has_res, splits, gate):
    a_refs = refs[:n_in]
    w_refs = refs[n_in:2 * n_in]
    idx = 2 * n_in
    g_ref = refs[idx] if has_norm else None
    idx += int(has_norm)
    r_ref = refs[idx] if has_res else None
    idx += int(has_res)
    out_refs = refs[idx:]
    acc = None
    for a_ref, w_ref in zip(a_refs, w_refs):
        a = a_ref[...]
        if has_norm:
            a = _rms(a, g_ref[...])
        d = _dot(a.astype(BF16), w_ref[...])
        acc = d if acc is None else acc + d
    if has_res:
        acc = acc + r_ref[...]
    if gate:
        d3 = acc.shape[1] // 3
        out_refs[0][...] = acc[:, :d3]
        out_refs[1][...] = acc[:, d3:2 * d3] * acc[:, 2 * d3:]
    else:
        off = 0
        for o_ref, s in zip(out_refs, splits):
            o_ref[...] = acc[:, off:off + s]
            off += s


def fused_linear(a_list, w_list, *, name, gain=None, res=None, splits=None, gate=False, tn=512):
    n = a_list[0].shape[0]
    m = w_list[0].shape[1]
    tn = min(tn, n)
    assert n % tn == 0
    if gate:
        out_cols = [m // 3, m // 3]
    else:
        out_cols = list(splits) if splits is not None else [m]
        assert sum(out_cols) == m
    in_specs, args = [], []
    for a in a_list:
        in_specs.append(pl.BlockSpec((tn, a.shape[1]), lambda i: (i, 0)))
        args.append(a)
    for w in w_list:
        in_specs.append(pl.BlockSpec(w.shape, lambda i: (0, 0)))
        args.append(w)
    if gain is not None:
        in_specs.append(pl.BlockSpec((1, gain.shape[-1]), lambda i: (0, 0)))
        args.append(gain.reshape(1, -1))
    if res is not None:
        in_specs.append(pl.BlockSpec((tn, m), lambda i: (i, 0)))
        args.append(res)
    out_shape = [jax.ShapeDtypeStruct((n, c), F32) for c in out_cols]
    out_specs = [pl.BlockSpec((tn, c), lambda i: (i, 0)) for c in out_cols]
    kern = functools.partial(_linear_kernel, n_in=len(a_list), has_norm=gain is not None,
                             has_res=res is not None, splits=tuple(out_cols), gate=gate)
    outs = pl.pallas_call(
        kern, name=name, grid=(n // tn,), in_specs=in_specs, out_specs=out_specs, out_shape=out_shape,
        compiler_params=_params(("parallel",)))(*args)
    return outs if len(outs) > 1 else outs[0]


def _memproj_kernel(a_ref, wk_ref, wv_ref, ok_ref, ov_ref):
    a = a_ref[...].astype(BF16)
    ok_ref[0] = _dot(a, wk_ref[0])
    ov_ref[0] = _dot(a, wv_ref[0])


def mem_project(mem2d, wk, wv):
    n, d = mem2d.shape
    depth = wk.shape[0]
    spec_w = pl.BlockSpec((1, d, d), lambda l: (l, 0, 0))
    spec_o = pl.BlockSpec((1, n, d), lambda l: (l, 0, 0))
    return pl.pallas_call(
        _memproj_kernel, name="mem_project", grid=(depth,),
        in_specs=[pl.BlockSpec((n, d), lambda l: (0, 0)), spec_w, spec_w],
        out_specs=[spec_o, spec_o],
        out_shape=[jax.ShapeDtypeStruct((depth, n, d), F32)] * 2,
        compiler_params=_params(("parallel",)))(mem2d, wk, wv)


def _lambda_value(lq1, lk1, lq2, lk2, lam_init):
    return (jnp.exp(jnp.sum(lq1[...] * lk1[...], axis=-1, keepdims=True))
            - jnp.exp(jnp.sum(lq2[...] * lk2[...], axis=-1, keepdims=True)) + lam_init)


def _subln(o, g, lam_init):
    return _rms(o, g) * (1.0 - lam_init)


def _dattn_prompt_kernel(lq1, lk1, lq2, lk2, g_ref, q_ref, k_ref, v_ref, o_ref, kb_scr, vt_scr, *,
                         tq, lam_init):
    qi = pl.program_id(2)
    nblk = kb_scr.shape[0]

    @pl.when(qi == 0)
    def _():
        for jb in range(nblk):
            kb_scr[jb] = k_ref[0, jb * tq:(jb + 1) * tq, :].astype(BF16)
            vt_scr[jb] = v_ref[0, jb * tq:(jb + 1) * tq, :].T.astype(BF16)

    qt = (q_ref[0] * (A_DH ** -0.5 * LOG2E)).T
    sub = lax.broadcasted_iota(jnp.int32, (A_HD, tq), 0)
    qz = (jnp.where(sub < A_DH, qt, 0.0).astype(BF16), jnp.where(sub >= A_DH, qt, 0.0).astype(BF16))

    def blocks(js, carry, masked):
        kbs = [kb_scr[j] for j in js]
        vts = [vt_scr[j] for j in js]
        out = []
        scores = [[_dot(kb, qz[mi]) for kb in kbs] for mi in range(2)]
        for mi in range(2):
            m, l, a = carry[3 * mi:3 * mi + 3]
            ss = scores[mi]
            if masked:
                kk = lax.broadcasted_iota(jnp.int32, (tq, tq), 0)
                qq = lax.broadcasted_iota(jnp.int32, (tq, tq), 1)
                ss = [jnp.where(kk <= qq, s, MASK_VALUE) for s in ss]
            m_new = functools.reduce(jnp.maximum, [m] + [jnp.max(s, axis=0, keepdims=True) for s in ss])
            alpha = jnp.exp2(m - m_new)
            l = alpha * l
            a = alpha * a
            for s, vt in zip(ss, vts):
                p = jnp.exp2(s - m_new)
                l = l + jnp.sum(p, axis=0, keepdims=True)
                a = a + _dot(vt, p.astype(BF16))
            out += [m_new, l, a]
        return tuple(out)

    init = (jnp.full((1, tq), NEG_INF, F32), jnp.zeros((1, tq), F32), jnp.zeros((A_HD, tq), F32)) * 2
    carry = lax.fori_loop(0, qi >> 1, lambda j, c: blocks((2 * j, 2 * j + 1), c, False), init)
    carry = lax.fori_loop(0, qi & 1, lambda j, c: blocks((qi - 1,), c, False), carry)
    m1, l1, a1, m2, l2, a2 = blocks((qi,), carry, True)
    lam = _lambda_value(lq1, lk1, lq2, lk2, lam_init)
    ot = a1 / l1 - lam * (a2 / l2)
    o_ref[0] = _subln(ot.T, g_ref[...], lam_init)


def dattn_prompt(q, k, v, lam_vecs, subln_g, lam_init, *, tq=512):
    b, t, _ = q.shape
    tq = min(tq, t)
    assert t % tq == 0
    vec = pl.BlockSpec((1, A_DH), lambda bi, h, qi: (0, 0))
    kern = functools.partial(_dattn_prompt_kernel, tq=tq, lam_init=lam_init)
    return pl.pallas_call(
        kern, name="dattn_prompt", grid=(b, A_HEADS, t // tq),
        in_specs=[vec, vec, vec, vec,
                  pl.BlockSpec((1, A_HD), lambda bi, h, qi: (0, 0)),
                  pl.BlockSpec((1, tq, A_HD), lambda bi, h, qi: (bi, qi, h)),
                  pl.BlockSpec((1, t, A_HD), lambda bi, h, qi: (bi, 0, h)),
                  pl.BlockSpec((1, t, A_HD), lambda bi, h, qi: (bi, 0, h))],
        out_specs=pl.BlockSpec((1, tq, A_HD), lambda bi, h, qi: (bi, qi, h)),
        out_shape=jax.ShapeDtypeStruct((b, t, A_QK), F32),
        scratch_shapes=[pltpu.VMEM((t // tq, tq, A_HD), BF16), pltpu.VMEM((t // tq, A_HD, tq), BF16)],
        compiler_params=_params(("parallel", "parallel", "arbitrary")))(
            *[x.reshape(1, A_DH) for x in lam_vecs], subln_g.reshape(1, A_HD), q, k, v)


def _dattn_sample_kernel(pt_ref, lq1, lk1, lq2, lk2, g_ref, q_ref, kn_ref, vn_ref, *rest,
                         n_pages, t_new, lam_init):
    k_refs = rest[:n_pages]
    v_refs = rest[n_pages:2 * n_pages]
    o_ref = rest[2 * n_pages]
    rows = A_HEADS * 2 * t_new
    hbits = A_HEADS.bit_length() - 1
    tbits = t_new.bit_length() - 1
    q = q_ref[0] * (A_DH ** -0.5)
    lane = lax.broadcasted_iota(jnp.int32, (t_new, A_HD), 1)
    parts = []
    for h in range(A_HEADS):
        qh = q[:, h * A_HD:(h + 1) * A_HD]
        parts += [jnp.where(lane < A_DH, qh, 0.0), jnp.where(lane >= A_DH, qh, 0.0)]
    qz = jnp.concatenate(parts, axis=0).astype(BF16)

    def bias(ncols, causal):
        rr = lax.broadcasted_iota(jnp.int32, (rows, ncols), 0)
        cc = lax.broadcasted_iota(jnp.int32, (rows, ncols), 1)
        ok = (cc & (A_HEADS - 1)) == (rr >> (tbits + 1))
        if causal:
            ok = ok & ((cc >> hbits) <= (rr & (t_new - 1)))
        return jnp.where(ok, 0.0, NEG_INF)

    page_bias = bias(PAGE_SIZE * A_HEADS, False)
    s_pages = [_dot_nt(qz, k_ref[0, 0].astype(BF16)) + page_bias for k_ref in k_refs]
    s_new = _dot_nt(qz, kn_ref[0].astype(BF16)) + bias(t_new * A_HEADS, True)
    m = jnp.max(functools.reduce(jnp.maximum, s_pages), axis=-1, keepdims=True)
    m = jnp.maximum(m, jnp.max(s_new, axis=-1, keepdims=True))
    p_new = jnp.exp(s_new - m)
    l = jnp.sum(p_new, axis=-1, keepdims=True)
    acc = _dot(p_new.astype(BF16), vn_ref[0].astype(BF16))
    for s, v_ref in zip(s_pages, v_refs):
        p = jnp.exp(s - m)
        l = l + jnp.sum(p, axis=-1, keepdims=True)
        acc = acc + _dot(p.astype(BF16), v_ref[0, 0].astype(BF16))
    acc = acc / l
    lam = _lambda_value(lq1, lk1, lq2, lk2, lam_init)
    for h in range(A_HEADS):
        r0 = h * 2 * t_new
        o = acc[r0:r0 + t_new] - lam * acc[r0 + t_new:r0 + 2 * t_new]
        o_ref[0, :, h * A_HD:(h + 1) * A_HD] = _subln(o, g_ref[...], lam_init)


def dattn_sample(q, k_new, v_new, cache_k, cache_v, layer, page_table, lam_vecs, subln_g, lam_init):
    b, t_new, _ = q.shape
    assert t_new & (t_new - 1) == 0
    n_pages = page_table.shape[1]
    pt = page_table.reshape(-1)
    vec = pl.BlockSpec((1, A_DH), lambda bi, pt: (0, 0))
    qspec = pl.BlockSpec((1, t_new, A_QK), lambda bi, pt: (bi, 0, 0))
    nspec = pl.BlockSpec((1, t_new * A_HEADS, A_HD), lambda bi, pt: (bi, 0, 0))

    def page(r):
        return pl.BlockSpec((1, 1, PAGE_SIZE * A_HEADS, A_HD),
                            lambda bi, pt: (layer, pt[bi * n_pages + r], 0, 0))

    pages = [page(r) for r in range(n_pages)]
    kern = functools.partial(_dattn_sample_kernel, n_pages=n_pages, t_new=t_new, lam_init=lam_init)
    grid_spec = pltpu.PrefetchScalarGridSpec(
        num_scalar_prefetch=1, grid=(b,),
        in_specs=[vec, vec, vec, vec, pl.BlockSpec((1, A_HD), lambda bi, pt: (0, 0)),
                  qspec, nspec, nspec] + pages + pages,
        out_specs=qspec)
    return pl.pallas_call(
        kern, name="dattn_sample", grid_spec=grid_spec,
        out_shape=jax.ShapeDtypeStruct((b, t_new, A_QK), F32),
        compiler_params=_params(("parallel",)))(
            pt, *[x.reshape(1, A_DH) for x in lam_vecs], subln_g.reshape(1, A_HD), q,
            k_new.reshape(b, t_new * A_HEADS, A_HD), v_new.reshape(b, t_new * A_HEADS, A_HD),
            *([cache_k] * n_pages), *([cache_v] * n_pages))


def _pool_kernel(p_ref, halo_ref, past_ref, w_ref, sc_ref, o_ref, scr, *, tt, pos0):
    ti = pl.program_id(1)
    scr[0:POOL_MAX, :] = jnp.where(ti == 0, past_ref[0], halo_ref[0])
    scr[POOL_MAX:, :] = p_ref[0]
    pos = pos0 + ti * tt + lax.broadcasted_iota(jnp.int32, (tt, POOL_GDIM), 0)
    for g, w in enumerate(POOL_WINDOWS):
        sl = slice(g * POOL_GDIM, (g + 1) * POOL_GDIM)
        x = scr[POOL_MAX:, sl]
        s = x
        for jj in range(1, w):
            s = s + scr[POOL_MAX - jj:POOL_MAX - jj + tt, sl]
        cnt = jnp.minimum(w, pos + 1).astype(F32)
        d = s / cnt - x
        y = _dot(d.astype(BF16), w_ref[g])
        o_ref[0, :, sl] = y * sc_ref[:, sl]


def pool_mix(p, past16, pos0, pool_w, pool_scale, *, tt=1024):
    b, t, _ = p.shape
    tt = min(tt, t)
    halo_src = p if t >= POOL_MAX else past16
    hb = tt // POOL_MAX
    kern = functools.partial(_pool_kernel, tt=tt, pos0=pos0)
    return pl.pallas_call(
        kern, name="pool_mix", grid=(b, t // tt),
        in_specs=[pl.BlockSpec((1, tt, POOL_DIM), lambda bi, ti: (bi, ti, 0)),
                  pl.BlockSpec((1, POOL_MAX, POOL_DIM), lambda bi, ti: (bi, jnp.maximum(ti * hb - 1, 0), 0)),
                  pl.BlockSpec((1, POOL_MAX, POOL_DIM), lambda bi, ti: (bi, 0, 0)),
                  pl.BlockSpec(pool_w.shape, lambda bi, ti: (0, 0, 0)),
                  pl.BlockSpec((1, POOL_DIM), lambda bi, ti: (0, 0))],
        out_specs=pl.BlockSpec((1, tt, POOL_DIM), lambda bi, ti: (bi, ti, 0)),
        out_shape=jax.ShapeDtypeStruct((b, t, POOL_DIM), F32),
        scratch_shapes=[pltpu.VMEM((POOL_MAX + tt, POOL_DIM), F32)],
        compiler_params=_params(("parallel", "arbitrary")))(
            p, halo_src, past16, pool_w, pool_scale.reshape(1, POOL_DIM))


CONV_HALO = 8


def _conv_kernel(u_ref, halo_ref, past_ref, b_ref, cw_ref, o_ref, scr, *, tt):
    ti = pl.program_id(1)
    scr[0:CONV_HALO, :] = jnp.where(ti == 0, past_ref[0], halo_ref[0])
    scr[CONV_HALO:, :] = u_ref[0]
    y = None
    for jj in range(CONV_W):
        off = CONV_HALO - (CONV_W - 1) + jj
        term = cw_ref[jj:jj + 1, :] * scr[off:off + tt, :]
        y = term if y is None else y + term
    o_ref[0] = b_ref[0] * y


def short_conv(u, bgate, past8, conv_w, *, tt=1024):
    b, t, d = u.shape
    tt = min(tt, t)
    hb = tt // CONV_HALO
    tile = pl.BlockSpec((1, tt, d), lambda bi, ti: (bi, ti, 0))
    kern = functools.partial(_conv_kernel, tt=tt)
    return pl.pallas_call(
        kern, name="short_conv", grid=(b, t // tt),
        in_specs=[tile,
                  pl.BlockSpec((1, CONV_HALO, d), lambda bi, ti: (bi, jnp.maximum(ti * hb - 1, 0), 0)),
                  pl.BlockSpec((1, CONV_HALO, d), lambda bi, ti: (bi, 0, 0)),
                  tile,
                  pl.BlockSpec((CONV_W, d), lambda bi, ti: (0, 0))],
        out_specs=tile,
        out_shape=jax.ShapeDtypeStruct((b, t, d), F32),
        scratch_shapes=[pltpu.VMEM((CONV_HALO + tt, d), F32)],
        compiler_params=_params(("parallel", "arbitrary")))(u, u, past8, bgate, conv_w)


def _memattn_kernel(q_ref, k_ref, v_ref, o_ref):
    scale = MEM_DH ** -0.5
    for h in range(MEM_HEADS):
        sl = slice(h * MEM_DH, (h + 1) * MEM_DH)
        qh = (q_ref[0, :, sl] * scale).astype(BF16)
        kh = k_ref[0, 0, :, sl].astype(BF16)
        vh = v_ref[0, 0, :, sl].astype(BF16)
        s = _dot_nt(qh, kh)
        s = s - jnp.max(s, axis=-1, keepdims=True)
        p = jnp.exp(s)
        p = p / jnp.sum(p, axis=-1, keepdims=True)
        o_ref[0, :, sl] = _dot(p.astype(BF16), vh)


def mem_attn(q, mk, mv, layer, *, tq=512):
    b, t, d = q.shape
    tq = min(tq, t)
    tile = pl.BlockSpec((1, tq, d), lambda bi, ti: (bi, ti, 0))
    mem = pl.BlockSpec((1, 1, N_MEM, d), lambda bi, ti: (layer, bi, 0, 0))
    return pl.pallas_call(
        _memattn_kernel, name="mem_attn", grid=(b, t // tq),
        in_specs=[tile, mem, mem], out_specs=tile,
        out_shape=jax.ShapeDtypeStruct((b, t, d), F32),
        compiler_params=_params(("parallel", "parallel")))(q, mk, mv)


MEM_SPLIT = MEM_DH // LANES
MEM_ROWS = MEM_HEADS * MEM_SPLIT


def _memattn_rows_kernel(q_ref, k_ref, v_ref, o_ref, *, t):
    ncol = N_MEM * MEM_ROWS
    q = q_ref[0] * (MEM_DH ** -0.5)
    qz = jnp.concatenate([q[:, kk * LANES:(kk + 1) * LANES] for kk in range(MEM_ROWS)], axis=0)
    sfull = _dot_nt(qz.astype(BF16), k_ref[0, 0].astype(BF16))
    sfull = sfull.reshape(MEM_HEADS, MEM_SPLIT, t, ncol)
    s = sfull[:, 0]
    for j in range(1, MEM_SPLIT):
        part = sfull[:, j].reshape(MEM_HEADS * t, ncol)
        s = s + pltpu.roll(part, ncol - j * MEM_HEADS, axis=1).reshape(MEM_HEADS, t, ncol)
    s = s.reshape(MEM_HEADS * t, ncol)
    rr = lax.broadcasted_iota(jnp.int32, (MEM_HEADS * t, ncol), 0)
    cc = lax.broadcasted_iota(jnp.int32, (MEM_HEADS * t, ncol), 1)
    valid = (cc & (MEM_ROWS - 1)) == (rr >> (t.bit_length() - 1))
    s = jnp.where(valid, s, NEG_INF)
    s = s - jnp.max(s, axis=-1, keepdims=True)
    p = jnp.exp(s)
    p = p / jnp.sum(p, axis=-1, keepdims=True)
    parts = [p.reshape(MEM_HEADS, t, ncol)]
    for j in range(1, MEM_SPLIT):
        parts.append(pltpu.roll(p, j * MEM_HEADS, axis=1).reshape(MEM_HEADS, t, ncol))
    pz = jnp.stack(parts, axis=1).reshape(MEM_ROWS * t, ncol)
    o = _dot(pz.astype(BF16), v_ref[0, 0].astype(BF16))
    for kk in range(MEM_ROWS):
        o_ref[0, :, kk * LANES:(kk + 1) * LANES] = o[kk * t:(kk + 1) * t]


def mem_attn_rows(q, mk_rows, mv_rows, layer):
    b, t, d = q.shape
    assert t & (t - 1) == 0 and t % SUBLANES == 0
    tile = pl.BlockSpec((1, t, d), lambda bi: (bi, 0, 0))
    mem = pl.BlockSpec((1, 1, N_MEM * MEM_ROWS, LANES), lambda bi: (layer, bi, 0, 0))
    return pl.pallas_call(
        functools.partial(_memattn_rows_kernel, t=t), name="mem_attn_rows", grid=(b,),
        in_specs=[tile, mem, mem], out_specs=tile,
        out_shape=jax.ShapeDtypeStruct((b, t, d), F32),
        compiler_params=_params(("parallel",)))(q, mk_rows, mv_rows)


def _mem_rows_view(cache):
    depth, b = cache.shape[:2]
    c = cache.reshape(depth, b, N_MEM, MEM_HEADS, MEM_SPLIT, LANES)
    return jnp.swapaxes(c, 3, 4).reshape(depth, b, N_MEM * MEM_ROWS, LANES)


PEER_DEPTH = PEER_TOPK + 1
PEER_VROWS = 24


def _sorting_network(n):
    comps = []

    def merge(lo, m, r):
        step = r * 2
        if step < m:
            merge(lo, m, step)
            merge(lo + r, m, step)
            comps.extend((i, i + r) for i in range(lo + r, lo + m - r, step))
        else:
            comps.append((lo, lo + r))

    def sort(lo, m):
        if m > 1:
            sort(lo, m // 2)
            sort(lo + m // 2, m // 2)
            merge(lo, m, 1)

    sort(0, n)
    return comps


def _peer_select_kernel(x_ref, g_ref, wqt_ref, sk_ref, ht_ref, d_ref, e1_ref, s2_ref, e2_ref,
                        qt_scr, v_scr, *, tt):
    h = _rms(x_ref[...], g_ref[...])
    ht = h.T.astype(BF16)
    ht_ref[...] = ht
    qt_scr[...] = _dot(wqt_ref[...], ht)
    half = PEER_TOPK // 2


    def pop_lists(lists, mask, depth):
        return [jnp.where(mask, lists[t + 1] if t + 1 < len(lists) else NEG_INF, lists[t])
                for t in range(depth)]

    def sorted_lists(s):
        v = [s[g * SUBLANES:(g + 1) * SUBLANES, :] for g in range(PEER_NKEYS // SUBLANES)]
        for i, j in _sorting_network(len(v)):
            v[i], v[j] = jnp.maximum(v[i], v[j]), jnp.minimum(v[i], v[j])
        return v

    def value_step(k, lists2, ls, vb):
        out = []
        for slot, v in enumerate(lists2):
            m = jnp.max(v[0], axis=0, keepdims=True)
            v_scr[vb + slot, k:k + 1, ls] = m
            if k + 1 < PEER_DEPTH:
                v = pop_lists(v, v[0] == m, PEER_TOPK - k)
            out.append(v)
        return out

    def candidates(ls, vb):
        v1 = v_scr[vb, :, ls]
        v2 = v_scr[vb + 1, :, ls]
        top = v1[0:half, :]
        first = jnp.where(lax.broadcasted_iota(jnp.int32, top.shape, 0) == 0, top, NEG_INF)
        lists = [(top if b < half else first) + v2[b:b + 1, :] for b in range(PEER_DEPTH)]
        singles = [v1[half:PEER_TOPK, :] + v2[0:1, :], v1[PEER_TOPK:PEER_VROWS, :] + v2[0:1, :]]
        return (lists, singles), v1[0:1, :], v2[0:1, :]

    def cand_step(k, state, tops):
        lists, singles = state
        m = jnp.max(functools.reduce(jnp.maximum, [lists[0]] + singles), axis=0, keepdims=True)
        tops.append(m)
        if k + 1 < PEER_DEPTH:
            lists = pop_lists(lists, lists[0] == m, PEER_TOPK - k)
            singles = [jnp.where(sg == m, NEG_INF, sg) for sg in singles]
        return lists, singles

    def finish(hh, tops, max1, max2, s1, s2, ls):
        z = functools.reduce(lambda a, b: a + b, [jnp.exp(t - tops[0]) for t in tops[:PEER_TOPK]])
        thr = 0.5 * (tops[PEER_TOPK - 1] + tops[PEER_TOPK])
        row = pl.multiple_of(hh * PEER_NKEYS, PEER_NKEYS)
        d_ref[pl.ds(row, PEER_NKEYS), ls] = thr - s1
        e1_ref[pl.ds(row, PEER_NKEYS), ls] = jnp.exp(s1 - max1) / z
        s2_ref[pl.ds(row, PEER_NKEYS), ls] = s2
        e2_ref[pl.ds(row, PEER_NKEYS), ls] = jnp.exp(s2 - max2)

    def head_pair(it, carry):
        head(2 * it, 0)
        head(2 * it + 1, 2)
        return carry

    def head(hh, vb):
        base = pl.multiple_of(hh * 2 * PEER_NKEYS, 2 * PEER_NKEYS)
        q1 = qt_scr[pl.ds(base, PEER_NKEYS), :].astype(BF16)
        q2 = qt_scr[pl.ds(base + PEER_NKEYS, PEER_NKEYS), :].astype(BF16)
        s1 = _dot(sk_ref[2 * hh], q1)
        s2 = _dot(sk_ref[2 * hh + 1], q2)
        lanes = [slice(t0, t0 + LANES) for t0 in range(0, tt, LANES)]
        prev = None
        for ls in lanes + [None]:
            ss = [sorted_lists(s1[:, ls]), sorted_lists(s2[:, ls])] if ls is not None else None
            if prev is not None:
                cands, max1, max2 = candidates(prev, vb)
                tops = []
            for k in range(PEER_DEPTH):
                if ss is not None:
                    ss = value_step(k, ss, ls, vb)
                if prev is not None:
                    cands = cand_step(k, cands, tops)
            if prev is not None:
                finish(hh, tops, max1, max2, s1[:, prev], s2[:, prev], prev)
            prev = ls

    v_scr[:, PEER_TOPK:, :] = jnp.full((4, PEER_VROWS - PEER_TOPK, tt), NEG_INF, F32)
    lax.fori_loop(0, PEER_HEADS // 2, head_pair, 0)


def peer_select(x, gain, wq_t, subkeys, *, tt=512):
    n, d = x.shape
    tt = min(tt, n)
    assert n % tt == 0
    rows = PEER_HEADS * PEER_NKEYS
    col = lambda r: pl.BlockSpec((r, tt), lambda i: (0, i))
    kern = functools.partial(_peer_select_kernel, tt=tt)
    return pl.pallas_call(
        kern, name="peer_select", grid=(n // tt,),
        in_specs=[pl.BlockSpec((tt, d), lambda i: (i, 0)),
                  pl.BlockSpec((1, d), lambda i: (0, 0)),
                  pl.BlockSpec(wq_t.shape, lambda i: (0, 0)),
                  pl.BlockSpec(subkeys.shape, lambda i: (0, 0, 0))],
        out_specs=[col(d), col(rows), col(rows), col(rows), col(rows)],
        out_shape=[jax.ShapeDtypeStruct((d, n), BF16)] + [jax.ShapeDtypeStruct((rows, n), F32)] * 4,
        scratch_shapes=[pltpu.VMEM((2 * rows, tt), F32), pltpu.VMEM((4, PEER_VROWS, tt), F32)],
        compiler_params=_params(("parallel",)))(x, gain.reshape(1, d), wq_t, subkeys)


PEER_CHUNK = 512
PEER_DENSE_FLAGS = None


def _peer_dense_kernel(x_ref, ht_ref, d_ref, e1_ref, s2_ref, e2_ref, u0_ref, un_ref, vtp_ref, vtl_ref, o_ref,
                       acc_scr, a0_scr, a1_scr, p0_scr, p1_scr, *, tt):
    a_scr = (a0_scr, a1_scr)
    p_scr = (p0_scr, p1_scr)
    c = pl.program_id(1)
    last = pl.num_programs(1) - 1
    rows_per_chunk = PEER_CHUNK // PEER_NKEYS
    groups = PEER_NKEYS // SUBLANES

    @pl.when(c == 0)
    def _():
        acc_scr[...] = jnp.zeros(acc_scr.shape, F32)
        p_scr[1][...] = jnp.zeros(p_scr[1].shape, BF16)
        a_scr[0][...] = _dot(u0_ref[...], ht_ref[...])

    def key_rows(r, ls):
        i = c * rows_per_chunk + r
        d_rows = [jnp.broadcast_to(d_ref[pl.ds(hh * PEER_NKEYS + i, 1), ls], (SUBLANES, MXU_WIDTH))
                  for hh in range(PEER_HEADS)]
        e_rows = [jnp.broadcast_to(e1_ref[pl.ds(hh * PEER_NKEYS + i, 1), ls], (SUBLANES, MXU_WIDTH))
                  for hh in range(PEER_HEADS)]
        return d_rows, e_rows

    def weights_item(r, g2, ls, cur, rows):
        d_rows, e_rows = rows
        vals = []
        for g in (g2, g2 + 1):
            w = None
            for hh in range(PEER_HEADS):
                js = slice(hh * PEER_NKEYS + g * SUBLANES, hh * PEER_NKEYS + (g + 1) * SUBLANES)
                sel = jnp.where(s2_ref[js, ls] >= d_rows[hh], e_rows[hh] * e2_ref[js, ls], 0.0)
                w = sel if w is None else w + sel
            a = a_scr[cur][r * PEER_NKEYS + g * SUBLANES:r * PEER_NKEYS + (g + 1) * SUBLANES, ls]
            vals.append(_gelu(a) * w)
        lo = r * PEER_NKEYS + g2 * SUBLANES
        p_scr[cur][lo:lo + 2 * SUBLANES, ls] = jnp.concatenate(vals, axis=0).astype(BF16)

    def preact_piece(k, ls, nxt):
        ks = slice(k * MXU_WIDTH, (k + 1) * MXU_WIDTH)
        part = _dot(un_ref[:, ks], ht_ref[ks, ls])
        if k == 0:
            a_scr[nxt][:, ls] = part
        else:
            a_scr[nxt][:, ls] += part

    def value_piece(k, ls, nxt):
        ks = slice(k * MXU_WIDTH, (k + 1) * MXU_WIDTH)
        acc_scr[:, ls] += _dot(vtp_ref[:, ks], p_scr[nxt][ks, ls])

    def step(cur, nxt):
        n_pre = ht_ref.shape[0] // MXU_WIDTH
        n_val = PEER_CHUNK // MXU_WIDTH
        merged = sorted([((k + 0.5) / n_pre, 0, preact_piece, k) for k in range(n_pre)]
                        + [((k + 0.5) / n_val, 1, value_piece, k) for k in range(n_val)],
                        key=lambda e: e[:2])
        pieces = {r: [] for r in range(rows_per_chunk)}
        for idx, (_, _, fn, k) in enumerate(merged):
            pieces[idx * rows_per_chunk // len(merged)].append((fn, k))
        for t0 in range(0, tt, MXU_WIDTH):
            ls = slice(t0, t0 + MXU_WIDTH)
            for r in range(rows_per_chunk):
                for fn, k in pieces[r]:
                    fn(k, ls, nxt)
                rows = key_rows(r, ls)
                for g2 in range(0, groups, 2):
                    weights_item(r, g2, ls, cur, rows)

    @pl.when((c & 1) == 0)
    def _():
        step(0, 1)

    @pl.when((c & 1) == 1)
    def _():
        step(1, 0)

    @pl.when(c == last)
    def _():
        acc = acc_scr[...] + _dot(vtl_ref[...], p_scr[1][...])
        o_ref[...] = x_ref[...] + acc.T


def peer_dense(x, ht, d_thr, e1, s2, e2, u_tab, v_tab_t, *, tt=512):
    n, d = x.shape
    tt = min(tt, n)
    assert n % tt == 0 and tt % MXU_WIDTH == 0
    rows = PEER_HEADS * PEER_NKEYS
    nchunks = PEER_EXPERTS // PEER_CHUNK
    assert nchunks % 2 == 0
    col = lambda r: pl.BlockSpec((r, tt), lambda i, c: (0, i))
    return pl.pallas_call(
        functools.partial(_peer_dense_kernel, tt=tt), name="peer_dense",
        grid=(n // tt, nchunks),
        in_specs=[pl.BlockSpec((tt, d), lambda i, c: (i, 0)),
                  col(d), col(rows), col(rows), col(rows), col(rows),
                  pl.BlockSpec((PEER_CHUNK, d), lambda i, c: (0, 0)),
                  pl.BlockSpec((PEER_CHUNK, d), lambda i, c: (jnp.minimum(c + 1, nchunks - 1), 0)),
                  pl.BlockSpec((d, PEER_CHUNK), lambda i, c: (0, jnp.maximum(c - 1, 0))),
                  pl.BlockSpec((d, PEER_CHUNK), lambda i, c: (0, nchunks - 1))],
        out_specs=pl.BlockSpec((tt, d), lambda i, c: (i, 0)),
        out_shape=jax.ShapeDtypeStruct((n, d), F32),
        scratch_shapes=[pltpu.VMEM((d, tt), F32)] + [pltpu.VMEM((PEER_CHUNK, tt), F32)] * 2
        + [pltpu.VMEM((PEER_CHUNK, tt), BF16)] * 2,
        compiler_params=_params(("parallel", "arbitrary"), flags=PEER_DENSE_FLAGS))(
            x, ht, d_thr, e1, s2, e2, u_tab, u_tab, v_tab_t, v_tab_t)


def _norm_kernel(x_ref, g_ref, o_ref):
    o_ref[...] = _rms(x_ref[...], g_ref[...])


def final_norm(x, gain, *, tn=512):
    n, d = x.shape
    tn = min(tn, n)
    assert n % tn == 0
    tile = pl.BlockSpec((tn, d), lambda i: (i, 0))
    return pl.pallas_call(
        _norm_kernel, name="final_norm", grid=(n // tn,),
        in_specs=[tile, pl.BlockSpec((1, d), lambda i: (0, 0))], out_specs=tile,
        out_shape=jax.ShapeDtypeStruct((n, d), F32),
        compiler_params=_params(("parallel",)))(x, gain.reshape(1, d))


def _trunk(x, pos0, mem_k, mem_v, pool_past, conv_past, cache_k, cache_v, page_table, W):
    b, t, d = x.shape
    n = b * t
    x = x.reshape(n, d)
    new_k, new_v, new_pool, new_conv = [], [], [], []
    for l in range(DEPTH):
        i = l // 2
        if l % 2 == 0:
            lam_init = 0.8 - 0.6 * math.exp(-0.3 * l)
            q, k, v, p = fused_linear([x], [W["w_in_ab"][i]], name="mix_in_ab", gain=W["norm_mix"][l],
                                      splits=[A_QK] * 4)
            q3, k3, v3, p3 = (a.reshape(b, t, A_QK) for a in (q, k, v, p))
            lam_vecs = [W[nm][i] for nm in ("lambda_q1", "lambda_k1", "lambda_q2", "lambda_k2")]
            if page_table is None:
                o = dattn_prompt(q3, k3, v3, lam_vecs, W["subln_g"][i], lam_init)
            else:
                o = dattn_sample(q3, k3, v3, cache_k, cache_v, i, page_table, lam_vecs,
                                 W["subln_g"][i], lam_init)
            past = pool_past[i]
            past16 = jnp.pad(past, ((0, 0), (1, 0), (0, 0)))
            pooled = pool_mix(p3, past16, pos0, W["pool_w"][i], W["pool_scale"][i])
            x = fused_linear([o.reshape(n, A_QK), pooled.reshape(n, POOL_DIM)],
                             [W["w_out_ab"][i][:A_QK], W["w_out_ab"][i][A_QK:]], name="mix_out_ab", res=x)
            new_k.append(k3.reshape(b, t, A_HEADS, A_HD))
            new_v.append(v3.reshape(b, t, A_HEADS, A_HD))
            new_pool.append(jnp.concatenate([past, p3], axis=1)[:, -(POOL_MAX - 1):])
        else:
            bg, u = fused_linear([x], [W["w_in_c"][i]], name="mix_in_c", gain=W["norm_mix"][l], gate=True)
            u3 = u.reshape(b, t, d)
            past = conv_past[i]
            past8 = jnp.pad(past, ((0, 0), (CONV_HALO - (CONV_W - 1), 0), (0, 0)))
            z = short_conv(u3, bg.reshape(b, t, d), past8, W["conv_w"][i])
            x = fused_linear([z.reshape(n, d)], [W["w_out_c"][i]], name="mix_out_c", res=x)
            new_conv.append(jnp.concatenate([past, u3], axis=1)[:, -(CONV_W - 1):])
        qm = fused_linear([x], [W["w_mq"][l]], name="mem_q", gain=W["norm_mem"][l])
        if page_table is None:
            om = mem_attn(qm.reshape(b, t, d), mem_k, mem_v, l)
        else:
            om = mem_attn_rows(qm.reshape(b, t, d), mem_k, mem_v, l)
        x = fused_linear([om.reshape(n, d)], [W["w_mo"][l]], name="mem_o", res=x)
        sel = peer_select(x, W["norm_ffn"][l], W["peer_wq_t"][l], W["peer_subkeys"][l])
        x = peer_dense(x, *sel, W["peer_u"][l], W["peer_v_t"][l])
    y = final_norm(x, W["norm_final"]).reshape(b, t, d)
    return y, jnp.stack(new_k), jnp.stack(new_v), jnp.stack(new_pool), jnp.stack(new_conv)


def _bf(w):
    return w.astype(BF16)


def _prepare_weights(**w):
    out = dict(w)
    for name in ("w_in_ab", "w_out_ab", "pool_w", "w_in_c", "w_out_c", "w_mq", "w_mo", "peer_u"):
        out[name] = _bf(w[name])
    out["peer_wq_t"] = _bf(jnp.swapaxes(w["peer_wq"], 1, 2))
    out["peer_subkeys"] = _bf(w["peer_subkeys"].reshape(DEPTH, PEER_HEADS * 2, PEER_NKEYS, PEER_NKEYS))
    out["peer_v_t"] = _bf(jnp.swapaxes(w["peer_v"], 1, 2))
    return out


def kernel(x_prompt, x_sample, cache_attn_k, cache_attn_v, state_pool, state_conv, cache_mem_k, cache_mem_v, page_table, mem_prompt, norm_mix, norm_mem, norm_ffn, norm_final, w_in_ab, w_out_ab, lambda_q1, lambda_k1, lambda_q2, lambda_k2, subln_g, pool_w, pool_scale, w_in_c, conv_w, w_out_c, w_mq, w_mk, w_mv, w_mo, peer_wq, peer_subkeys, peer_u, peer_v):
    W = _prepare_weights(
        norm_mix=norm_mix, norm_mem=norm_mem, norm_ffn=norm_ffn, norm_final=norm_final, w_in_ab=w_in_ab,
        w_out_ab=w_out_ab, lambda_q1=lambda_q1, lambda_k1=lambda_k1, lambda_q2=lambda_q2, lambda_k2=lambda_k2,
        subln_g=subln_g, pool_w=pool_w, pool_scale=pool_scale, w_in_c=w_in_c, conv_w=conv_w, w_out_c=w_out_c,
        w_mq=w_mq, w_mo=w_mo, peer_wq=peer_wq, peer_subkeys=peer_subkeys, peer_u=peer_u, peer_v=peer_v)
    bp, sp, d = x_prompt.shape
    mk_p, mv_p = mem_project(mem_prompt.reshape(bp * N_MEM, d), _bf(w_mk), _bf(w_mv))
    mk_p = mk_p.reshape(DEPTH, bp, N_MEM, d)
    mv_p = mv_p.reshape(DEPTH, bp, N_MEM, d)
    pool0 = jnp.zeros((w_in_ab.shape[0], bp, POOL_MAX - 1, POOL_DIM), F32)
    conv0 = jnp.zeros((w_in_c.shape[0], bp, CONV_W - 1, d), F32)
    y_p, k_p, v_p, pool_p, conv_p = _trunk(x_prompt, 0, mk_p, mv_p, pool0, conv0, None, None, None, W)
    past_len = page_table.shape[1] * cache_attn_k.shape[2]
    n_layers, n_pool = cache_attn_k.shape[:2]
    ck = cache_attn_k.reshape(n_layers, n_pool, PAGE_SIZE * A_HEADS, A_HD)
    cv = cache_attn_v.reshape(n_layers, n_pool, PAGE_SIZE * A_HEADS, A_HD)
    y_s, k_s, v_s, pool_s, conv_s = _trunk(
        x_sample, past_len, _mem_rows_view(cache_mem_k), _mem_rows_view(cache_mem_v), state_pool, state_conv,
        ck, cv, page_table, W)
    mshape = (DEPTH, bp, N_MEM, MEM_HEADS, MEM_DH)
    return (y_p, y_s, k_p, v_p, pool_p, conv_p, mk_p.reshape(mshape), mv_p.reshape(mshape),
            k_s, v_s, pool_s, conv_s)
```

```python
import functools
import math

import jax
import jax.numpy as jnp
from jax import lax
from jax.experimental import pallas as pl
from jax.experimental.pallas import tpu as pltpu

F32 = jnp.float32
BF16 = jnp.bfloat16

D_MODEL = 1024
DEPTH = 4
EPS = 1e-5
A_HEADS = 4
A_DH = 64
A_HD = 2 * A_DH
A_QK = A_HEADS * A_HD
POOL_WINDOWS = (2, 4, 8, 16)
POOL_GDIM = 128
POOL_DIM = 512
POOL_MAX = 16
CONV_W = 3
N_MEM = 256
MEM_HEADS = 4
MEM_DH = 256
PEER_HEADS = 8
PEER_NKEYS = 128
PEER_EXPERTS = PEER_NKEYS * PEER_NKEYS
PEER_TOPK = 16
PAGE_SIZE = 128

SUBLANES = 8
LANES = 128
MXU_WIDTH = 256
VMEM_LIMIT = 56 * 1024 * 1024
NEG_INF = float("-inf")
MASK_VALUE = float(jnp.finfo(jnp.float32).min)
LOG2E = 1.4426950408889634


def _params(sem, vmem=VMEM_LIMIT, flags=None):
    return pltpu.CompilerParams(dimension_semantics=sem, vmem_limit_bytes=vmem, flags=flags)


def _rms(x, g):
    return x * lax.rsqrt(jnp.mean(x * x, axis=-1, keepdims=True) + EPS) * g


def _gelu(a):
    return 0.5 * a * (1.0 + lax.erf(a * (2.0 ** -0.5)))


def _dot(a, b):
    return jnp.dot(a, b, preferred_element_type=F32)


def _dot_nt(a, b):
    return lax.dot_general(a, b, (((1,), (1,)), ((), ())), preferred_element_type=F32)


def _linear_kernel(*refs, n_in, has_norm, has_res, splits, gate):
    a_refs = refs[:n_in]
    w_refs = refs[n_in:2 * n_in]
    idx = 2 * n_in
    g_ref = refs[idx] if has_norm else None
    idx += int(has_norm)
    r_ref = refs[idx] if has_res else None
    idx += int(has_res)
    out_refs = refs[idx:]
    acc = None
    for a_ref, w_ref in zip(a_refs, w_refs):
        a = a_ref[...]
        if has_norm:
            a = _rms(a, g_ref[...])
        d = _dot(a.astype(BF16), w_ref[...])
        acc = d if acc is None else acc + d
    if has_res:
        acc = acc + r_ref[...]
    if gate:
        d3 = acc.shape[1] // 3
        out_refs[0][...] = acc[:, :d3]
        out_refs[1][...] = acc[:, d3:2 * d3] * acc[:, 2 * d3:]
    else:
        off = 0
        for o_ref, s in zip(out_refs, splits):
            o_ref[...] = acc[:, off:off + s]
            off += s


def fused_linear(a_list, w_list, *, name, gain=None, res=None, splits=None, gate=False, tn=512):
    n = a_list[0].shape[0]
    m = w_list[0].shape[1]
    tn = min(tn, n)
    assert n % tn == 0
    if gate:
        out_cols = [m // 3, m // 3]
    else:
        out_cols = list(splits) if splits is not None else [m]
        assert sum(out_cols) == m
    in_specs, args = [], []
    for a in a_list:
        in_specs.append(pl.BlockSpec((tn, a.shape[1]), lambda i: (i, 0)))
        args.append(a)
    for w in w_list:
        in_specs.append(pl.BlockSpec(w.shape, lambda i: (0, 0)))
        args.append(w)
    if gain is not None:
        in_specs.append(pl.BlockSpec((1, gain.shape[-1]), lambda i: (0, 0)))
        args.append(gain.reshape(1, -1))
    if res is not None:
        in_specs.append(pl.BlockSpec((tn, m), lambda i: (i, 0)))
        args.append(res)
    out_shape = [jax.ShapeDtypeStruct((n, c), F32) for c in out_cols]
    out_specs = [pl.BlockSpec((tn, c), lambda i: (i, 0)) for c in out_cols]
    kern = functools.partial(_linear_kernel, n_in=len(a_list), has_norm=gain is not None,
                             has_res=res is not None, splits=tuple(out_cols), gate=gate)
    outs = pl.pallas_call(
        kern, name=name, grid=(n // tn,), in_specs=in_specs, out_specs=out_specs, out_shape=out_shape,
        compiler_params=_params(("parallel",)))(*args)
    return outs if len(outs) > 1 else outs[0]


def _memproj_kernel(a_ref, wk_ref, wv_ref, ok_ref, ov_ref):
    a = a_ref[...].astype(BF16)
    ok_ref[0] = _dot(a, wk_ref[0])
    ov_ref[0] = _dot(a, wv_ref[0])


def mem_project(mem2d, wk, wv):
    n, d = mem2d.shape
    depth = wk.shape[0]
    spec_w = pl.BlockSpec((1, d, d), lambda l: (l, 0, 0))
    spec_o = pl.BlockSpec((1, n, d), lambda l: (l, 0, 0))
    return pl.pallas_call(
        _memproj_kernel, name="mem_project", grid=(depth,),
        in_specs=[pl.BlockSpec((n, d), lambda l: (0, 0)), spec_w, spec_w],
        out_specs=[spec_o, spec_o],
        out_shape=[jax.ShapeDtypeStruct((depth, n, d), F32)] * 2,
        compiler_params=_params(("parallel",)))(mem2d, wk, wv)


def _lambda_value(lq1, lk1, lq2, lk2, lam_init):
    return (jnp.exp(jnp.sum(lq1[...] * lk1[...], axis=-1, keepdims=True))
            - jnp.exp(jnp.sum(lq2[...] * lk2[...], axis=-1, keepdims=True)) + lam_init)


def _subln(o, g, lam_init):
    return _rms(o, g) * (1.0 - lam_init)


def _dattn_prompt_kernel(lq1, lk1, lq2, lk2, g_ref, q_ref, k_ref, v_ref, o_ref, kb_scr, vt_scr, *,
                         tq, lam_init):
    qi = pl.program_id(2)
    nblk = kb_scr.shape[0]

    @pl.when(qi == 0)
    def _():
        for jb in range(nblk):
            kb_scr[jb] = k_ref[0, jb * tq:(jb + 1) * tq, :].astype(BF16)
            vt_scr[jb] = v_ref[0, jb * tq:(jb + 1) * tq, :].T.astype(BF16)

    qt = (q_ref[0] * (A_DH ** -0.5 * LOG2E)).T
    sub = lax.broadcasted_iota(jnp.int32, (A_HD, tq), 0)
    qz = (jnp.where(sub < A_DH, qt, 0.0).astype(BF16), jnp.where(sub >= A_DH, qt, 0.0).astype(BF16))

    def blocks(js, carry, masked):
        kbs = [kb_scr[j] for j in js]
        vts = [vt_scr[j] for j in js]
        out = []
        scores = [[_dot(kb, qz[mi]) for kb in kbs] for mi in range(2)]
        for mi in range(2):
            m, l, a = carry[3 * mi:3 * mi + 3]
            ss = scores[mi]
            if masked:
                kk = lax.broadcasted_iota(jnp.int32, (tq, tq), 0)
                qq = lax.broadcasted_iota(jnp.int32, (tq, tq), 1)
                ss = [jnp.where(kk <= qq, s, MASK_VALUE) for s in ss]
            m_new = functools.reduce(jnp.maximum, [m] + [jnp.max(s, axis=0, keepdims=True) for s in ss])
            alpha = jnp.exp2(m - m_new)
            l = alpha * l
            a = alpha * a
            for s, vt in zip(ss, vts):
                p = jnp.exp2(s - m_new)
                l = l + jnp.sum(p, axis=0, keepdims=True)
                a = a + _dot(vt, p.astype(BF16))
            out += [m_new, l, a]
        return tuple(out)

    init = (jnp.full((1, tq), NEG_INF, F32), jnp.zeros((1, tq), F32), jnp.zeros((A_HD, tq), F32)) * 2
    carry = lax.fori_loop(0, qi >> 1, lambda j, c: blocks((2 * j, 2 * j + 1), c, False), init)
    carry = lax.fori_loop(0, qi & 1, lambda j, c: blocks((qi - 1,), c, False), carry)
    m1, l1, a1, m2, l2, a2 = blocks((qi,), carry, True)
    lam = _lambda_value(lq1, lk1, lq2, lk2, lam_init)
    ot = a1 / l1 - lam * (a2 / l2)
    o_ref[0] = _subln(ot.T, g_ref[...], lam_init)


def dattn_prompt(q, k, v, lam_vecs, subln_g, lam_init, *, tq=512):
    b, t, _ = q.shape
    tq = min(tq, t)
    assert t % tq == 0
    vec = pl.BlockSpec((1, A_DH), lambda bi, h, qi: (0, 0))
    kern = functools.partial(_dattn_prompt_kernel, tq=tq, lam_init=lam_init)
    return pl.pallas_call(
        kern, name="dattn_prompt", grid=(b, A_HEADS, t // tq),
        in_specs=[vec, vec, vec, vec,
                  pl.BlockSpec((1, A_HD), lambda bi, h, qi: (0, 0)),
                  pl.BlockSpec((1, tq, A_HD), lambda bi, h, qi: (bi, qi, h)),
                  pl.BlockSpec((1, t, A_HD), lambda bi, h, qi: (bi, 0, h)),
                  pl.BlockSpec((1, t, A_HD), lambda bi, h, qi: (bi, 0, h))],
        out_specs=pl.BlockSpec((1, tq, A_HD), lambda bi, h, qi: (bi, qi, h)),
        out_shape=jax.ShapeDtypeStruct((b, t, A_QK), F32),
        scratch_shapes=[pltpu.VMEM((t // tq, tq, A_HD), BF16), pltpu.VMEM((t // tq, A_HD, tq), BF16)],
        compiler_params=_params(("parallel", "parallel", "arbitrary")))(
            *[x.reshape(1, A_DH) for x in lam_vecs], subln_g.reshape(1, A_HD), q, k, v)


def _dattn_sample_kernel(pt_ref, lq1, lk1, lq2, lk2, g_ref, q_ref, kn_ref, vn_ref, *rest,
                         n_pages, t_new, lam_init):
    k_refs = rest[:n_pages]
    v_refs = rest[n_pages:2 * n_pages]
    o_ref = rest[2 * n_pages]
    rows = A_HEADS * 2 * t_new
    hbits = A_HEADS.bit_length() - 1
    tbits = t_new.bit_length() - 1
    q = q_ref[0] * (A_DH ** -0.5)
    lane = lax.broadcasted_iota(jnp.int32, (t_new, A_HD), 1)
    parts = []
    for h in range(A_HEADS):
        qh = q[:, h * A_HD:(h + 1) * A_HD]
        parts += [jnp.where(lane < A_DH, qh, 0.0), jnp.where(lane >= A_DH, qh, 0.0)]
    qz = jnp.concatenate(parts, axis=0).astype(BF16)

    def bias(ncols, causal):
        rr = lax.broadcasted_iota(jnp.int32, (rows, ncols), 0)
        cc = lax.broadcasted_iota(jnp.int32, (rows, ncols), 1)
        ok = (cc & (A_HEADS - 1)) == (rr >> (tbits + 1))
        if causal:
            ok = ok & ((cc >> hbits) <= (rr & (t_new - 1)))
        return jnp.where(ok, 0.0, NEG_INF)

    page_bias = bias(PAGE_SIZE * A_HEADS, False)
    s_pages = [_dot_nt(qz, k_ref[0, 0].astype(BF16)) + page_bias for k_ref in k_refs]
    s_new = _dot_nt(qz, kn_ref[0].astype(BF16)) + bias(t_new * A_HEADS, True)
    m = jnp.max(functools.reduce(jnp.maximum, s_pages), axis=-1, keepdims=True)
    m = jnp.maximum(m, jnp.max(s_new, axis=-1, keepdims=True))
    p_new = jnp.exp(s_new - m)
    l = jnp.sum(p_new, axis=-1, keepdims=True)
    acc = _dot(p_new.astype(BF16), vn_ref[0].astype(BF16))
    for s, v_ref in zip(s_pages, v_refs):
        p = jnp.exp(s - m)
        l = l + jnp.sum(p, axis=-1, keepdims=True)
        acc = acc + _dot(p.astype(BF16), v_ref[0, 0].astype(BF16))
    acc = acc / l
    lam = _lambda_value(lq1, lk1, lq2, lk2, lam_init)
    for h in range(A_HEADS):
        r0 = h * 2 * t_new
        o = acc[r0:r0 + t_new] - lam * acc[r0 + t_new:r0 + 2 * t_new]
        o_ref[0, :, h * A_HD:(h + 1) * A_HD] = _subln(o, g_ref[...], lam_init)


def dattn_sample(q, k_new, v_new, cache_k, cache_v, layer, page_table, lam_vecs, subln_g, lam_init):
    b, t_new, _ = q.shape
    assert t_new & (t_new - 1) == 0
    n_pages = page_table.shape[1]
    pt = page_table.reshape(-1)
    vec = pl.BlockSpec((1, A_DH), lambda bi, pt: (0, 0))
    qspec = pl.BlockSpec((1, t_new, A_QK), lambda bi, pt: (bi, 0, 0))
    nspec = pl.BlockSpec((1, t_new * A_HEADS, A_HD), lambda bi, pt: (bi, 0, 0))

    def page(r):
        return pl.BlockSpec((1, 1, PAGE_SIZE * A_HEADS, A_HD),
                            lambda bi, pt: (layer, pt[bi * n_pages + r], 0, 0))

    pages = [page(r) for r in range(n_pages)]
    kern = functools.partial(_dattn_sample_kernel, n_pages=n_pages, t_new=t_new, lam_init=lam_init)
    grid_spec = pltpu.PrefetchScalarGridSpec(
        num_scalar_prefetch=1, grid=(b,),
        in_specs=[vec, vec, vec, vec, pl.BlockSpec((1, A_HD), lambda bi, pt: (0, 0)),
                  qspec, nspec, nspec] + pages + pages,
        out_specs=qspec)
    return pl.pallas_call(
        kern, name="dattn_sample", grid_spec=grid_spec,
        out_shape=jax.ShapeDtypeStruct((b, t_new, A_QK), F32),
        compiler_params=_params(("parallel",)))(
            pt, *[x.reshape(1, A_DH) for x in lam_vecs], subln_g.reshape(1, A_HD), q,
            k_new.reshape(b, t_new * A_HEADS, A_HD), v_new.reshape(b, t_new * A_HEADS, A_HD),
            *([cache_k] * n_pages), *([cache_v] * n_pages))


def _pool_kernel(p_ref, halo_ref, past_ref, w_ref, sc_ref, o_ref, scr, *, tt, pos0):
    ti = pl.program_id(1)
    scr[0:POOL_MAX, :] = jnp.where(ti == 0, past_ref[0], halo_ref[0])
    scr[POOL_MAX:, :] = p_ref[0]
    pos = pos0 + ti * tt + lax.broadcasted_iota(jnp.int32, (tt, POOL_GDIM), 0)
    for g, w in enumerate(POOL_WINDOWS):
        sl = slice(g * POOL_GDIM, (g + 1) * POOL_GDIM)
        x = scr[POOL_MAX:, sl]
        s = x
        for jj in range(1, w):
            s = s + scr[POOL_MAX - jj:POOL_MAX - jj + tt, sl]
        cnt = jnp.minimum(w, pos + 1).astype(F32)
        d = s / cnt - x
        y = _dot(d.astype(BF16), w_ref[g])
        o_ref[0, :, sl] = y * sc_ref[:, sl]


def pool_mix(p, past16, pos0, pool_w, pool_scale, *, tt=1024):
    b, t, _ = p.shape
    tt = min(tt, t)
    halo_src = p if t >= POOL_MAX else past16
    hb = tt // POOL_MAX
    kern = functools.partial(_pool_kernel, tt=tt, pos0=pos0)
    return pl.pallas_call(
        kern, name="pool_mix", grid=(b, t // tt),
        in_specs=[pl.BlockSpec((1, tt, POOL_DIM), lambda bi, ti: (bi, ti, 0)),
                  pl.BlockSpec((1, POOL_MAX, POOL_DIM), lambda bi, ti: (bi, jnp.maximum(ti * hb - 1, 0), 0)),
                  pl.BlockSpec((1, POOL_MAX, POOL_DIM), lambda bi, ti: (bi, 0, 0)),
                  pl.BlockSpec(pool_w.shape, lambda bi, ti: (0, 0, 0)),
                  pl.BlockSpec((1, POOL_DIM), lambda bi, ti: (0, 0))],
        out_specs=pl.BlockSpec((1, tt, POOL_DIM), lambda bi, ti: (bi, ti, 0)),
        out_shape=jax.ShapeDtypeStruct((b, t, POOL_DIM), F32),
        scratch_shapes=[pltpu.VMEM((POOL_MAX + tt, POOL_DIM), F32)],
        compiler_params=_params(("parallel", "arbitrary")))(
            p, halo_src, past16, pool_w, pool_scale.reshape(1, POOL_DIM))


CONV_HALO = 8


def _conv_kernel(u_ref, halo_ref, past_ref, b_ref, cw_ref, o_ref, scr, *, tt):
    ti = pl.program_id(1)
    scr[0:CONV_HALO, :] = jnp.where(ti == 0, past_ref[0], halo_ref[0])
    scr[CONV_HALO:, :] = u_ref[0]
    y = None
    for jj in range(CONV_W):
        off = CONV_HALO - (CONV_W - 1) + jj
        term = cw_ref[jj:jj + 1, :] * scr[off:off + tt, :]
        y = term if y is None else y + term
    o_ref[0] = b_ref[0] * y


def short_conv(u, bgate, past8, conv_w, *, tt=1024):
    b, t, d = u.shape
    tt = min(tt, t)
    hb = tt // CONV_HALO
    tile = pl.BlockSpec((1, tt, d), lambda bi, ti: (bi, ti, 0))
    kern = functools.partial(_conv_kernel, tt=tt)
    return pl.pallas_call(
        kern, name="short_conv", grid=(b, t // tt),
        in_specs=[tile,
                  pl.BlockSpec((1, CONV_HALO, d), lambda bi, ti: (bi, jnp.maximum(ti * hb - 1, 0), 0)),
                  pl.BlockSpec((1, CONV_HALO, d), lambda bi, ti: (bi, 0, 0)),
                  tile,
                  pl.BlockSpec((CONV_W, d), lambda bi, ti: (0, 0))],
        out_specs=tile,
        out_shape=jax.ShapeDtypeStruct((b, t, d), F32),
        scratch_shapes=[pltpu.VMEM((CONV_HALO + tt, d), F32)],
        compiler_params=_params(("parallel", "arbitrary")))(u, u, past8, bgate, conv_w)


def _memattn_kernel(q_ref, k_ref, v_ref, o_ref):
    scale = MEM_DH ** -0.5
    for h in range(MEM_HEADS):
        sl = slice(h * MEM_DH, (h + 1) * MEM_DH)
        qh = (q_ref[0, :, sl] * scale).astype(BF16)
        kh = k_ref[0, 0, :, sl].astype(BF16)
        vh = v_ref[0, 0, :, sl].astype(BF16)
        s = _dot_nt(qh, kh)
        s = s - jnp.max(s, axis=-1, keepdims=True)
        p = jnp.exp(s)
        p = p / jnp.sum(p, axis=-1, keepdims=True)
        o_ref[0, :, sl] = _dot(p.astype(BF16), vh)


def mem_attn(q, mk, mv, layer, *, tq=512):
    b, t, d = q.shape
    tq = min(tq, t)
    tile = pl.BlockSpec((1, tq, d), lambda bi, ti: (bi, ti, 0))
    mem = pl.BlockSpec((1, 1, N_MEM, d), lambda bi, ti: (layer, bi, 0, 0))
    return pl.pallas_call(
        _memattn_kernel, name="mem_attn", grid=(b, t // tq),
        in_specs=[tile, mem, mem], out_specs=tile,
        out_shape=jax.ShapeDtypeStruct((b, t, d), F32),
        compiler_params=_params(("parallel", "parallel")))(q, mk, mv)


MEM_SPLIT = MEM_DH // LANES
MEM_ROWS = MEM_HEADS * MEM_SPLIT


def _memattn_rows_kernel(q_ref, k_ref, v_ref, o_ref, *, t):
    ncol = N_MEM * MEM_ROWS
    q = q_ref[0] * (MEM_DH ** -0.5)
    qz = jnp.concatenate([q[:, kk * LANES:(kk + 1) * LANES] for kk in range(MEM_ROWS)], axis=0)
    sfull = _dot_nt(qz.astype(BF16), k_ref[0, 0].astype(BF16))
    sfull = sfull.reshape(MEM_HEADS, MEM_SPLIT, t, ncol)
    s = sfull[:, 0]
    for j in range(1, MEM_SPLIT):
        part = sfull[:, j].reshape(MEM_HEADS * t, ncol)
        s = s + pltpu.roll(part, ncol - j * MEM_HEADS, axis=1).reshape(MEM_HEADS, t, ncol)
    s = s.reshape(MEM_HEADS * t, ncol)
    rr = lax.broadcasted_iota(jnp.int32, (MEM_HEADS * t, ncol), 0)
    cc = lax.broadcasted_iota(jnp.int32, (MEM_HEADS * t, ncol), 1)
    valid = (cc & (MEM_ROWS - 1)) == (rr >> (t.bit_length() - 1))
    s = jnp.where(valid, s, NEG_INF)
    s = s - jnp.max(s, axis=-1, keepdims=True)
    p = jnp.exp(s)
    p = p / jnp.sum(p, axis=-1, keepdims=True)
    parts = [p.reshape(MEM_HEADS, t, ncol)]
    for j in range(1, MEM_SPLIT):
        parts.append(pltpu.roll(p, j * MEM_HEADS, axis=1).reshape(MEM_HEADS, t, ncol))
    pz = jnp.stack(parts, axis=1).reshape(MEM_ROWS * t, ncol)
    o = _dot(pz.astype(BF16), v_ref[0, 0].astype(BF16))
    for kk in range(MEM_ROWS):
        o_ref[0, :, kk * LANES:(kk + 1) * LANES] = o[kk * t:(kk + 1) * t]


def mem_attn_rows(q, mk_rows, mv_rows, layer):
    b, t, d = q.shape
    assert t & (t - 1) == 0 and t % SUBLANES == 0
    tile = pl.BlockSpec((1, t, d), lambda bi: (bi, 0, 0))
    mem = pl.BlockSpec((1, 1, N_MEM * MEM_ROWS, LANES), lambda bi: (layer, bi, 0, 0))
    return pl.pallas_call(
        functools.partial(_memattn_rows_kernel, t=t), name="mem_attn_rows", grid=(b,),
        in_specs=[tile, mem, mem], out_specs=tile,
        out_shape=jax.ShapeDtypeStruct((b, t, d), F32),
        compiler_params=_params(("parallel",)))(q, mk_rows, mv_rows)


def _mem_rows_view(cache):
    depth, b = cache.shape[:2]
    c = cache.reshape(depth, b, N_MEM, MEM_HEADS, MEM_SPLIT, LANES)
    return jnp.swapaxes(c, 3, 4).reshape(depth, b, N_MEM * MEM_ROWS, LANES)


PEER_ROWS = PEER_HEADS * PEER_NKEYS
PEER_DEPTH = PEER_TOPK + 1
PEER_VROWS = 24


def _sorting_network(n):
    comps = []

    def merge(lo, m, r):
        step = r * 2
        if step < m:
            merge(lo, m, step)
            merge(lo + r, m, step)
            comps.extend((i, i + r) for i in range(lo + r, lo + m - r, step))
        else:
            comps.append((lo, lo + r))

    def sort(lo, m):
        if m > 1:
            sort(lo, m // 2)
            sort(lo + m // 2, m // 2)
            merge(lo, m, 1)

    sort(0, n)
    return comps


def _peer_select_kernel(x_ref, g_ref, wqt_ref, sk_ref, ht_ref, d_ref, e1_ref, s2_ref, e2_ref,
                        qt_scr, v_scr, *, tt):
    h = _rms(x_ref[...], g_ref[...])
    ht = h.T.astype(BF16)
    ht_ref[...] = ht
    qt_scr[...] = _dot(wqt_ref[...], ht)
    half = PEER_TOPK // 2


    def pop_lists(lists, mask, depth):
        return [jnp.where(mask, lists[t + 1] if t + 1 < len(lists) else NEG_INF, lists[t])
                for t in range(depth)]

    def sorted_lists(s):
        v = [s[g * SUBLANES:(g + 1) * SUBLANES, :] for g in range(PEER_NKEYS // SUBLANES)]
        for i, j in _sorting_network(len(v)):
            v[i], v[j] = jnp.maximum(v[i], v[j]), jnp.minimum(v[i], v[j])
        return v

    def value_step(k, lists2, ls, vb):
        out = []
        for slot, v in enumerate(lists2):
            m = jnp.max(v[0], axis=0, keepdims=True)
            v_scr[vb + slot, k:k + 1, ls] = m
            if k + 1 < PEER_DEPTH:
                v = pop_lists(v, v[0] == m, PEER_TOPK - k)
            out.append(v)
        return out

    def candidates(ls, vb):
        v1 = v_scr[vb, :, ls]
        v2 = v_scr[vb + 1, :, ls]
        top = v1[0:half, :]
        first = jnp.where(lax.broadcasted_iota(jnp.int32, top.shape, 0) == 0, top, NEG_INF)
        lists = [(top if b < half else first) + v2[b:b + 1, :] for b in range(PEER_DEPTH)]
        singles = [v1[half:PEER_TOPK, :] + v2[0:1, :], v1[PEER_TOPK:PEER_VROWS, :] + v2[0:1, :]]
        return (lists, singles), v1[0:1, :], v2[0:1, :]

    def cand_step(k, state, tops):
        lists, singles = state
        m = jnp.max(functools.reduce(jnp.maximum, [lists[0]] + singles), axis=0, keepdims=True)
        tops.append(m)
        if k + 1 < PEER_DEPTH:
            lists = pop_lists(lists, lists[0] == m, PEER_TOPK - k)
            singles = [jnp.where(sg == m, NEG_INF, sg) for sg in singles]
        return lists, singles

    def finish(hh, tops, max1, max2, s1, s2, ls):
        z = functools.reduce(lambda a, b: a + b, [jnp.exp(t - tops[0]) for t in tops[:PEER_TOPK]])
        thr = 0.5 * (tops[PEER_TOPK - 1] + tops[PEER_TOPK])
        row = pl.multiple_of((ls.start // LANES) * PEER_ROWS + hh * PEER_NKEYS, PEER_NKEYS)
        d_ref[pl.ds(row, PEER_NKEYS), :] = thr - s1
        e1_ref[pl.ds(row, PEER_NKEYS), :] = jnp.exp(s1 - max1) / z
        s2_ref[pl.ds(row, PEER_NKEYS), :] = s2
        e2_ref[pl.ds(row, PEER_NKEYS), :] = jnp.exp(s2 - max2)

    def head_pair(it, carry):
        head(2 * it, 0)
        head(2 * it + 1, 2)
        return carry

    def head(hh, vb):
        base = pl.multiple_of(hh * 2 * PEER_NKEYS, 2 * PEER_NKEYS)
        q1 = qt_scr[pl.ds(base, PEER_NKEYS), :].astype(BF16)
        q2 = qt_scr[pl.ds(base + PEER_NKEYS, PEER_NKEYS), :].astype(BF16)
        s1 = _dot(sk_ref[2 * hh], q1)
        s2 = _dot(sk_ref[2 * hh + 1], q2)
        lanes = [slice(t0, t0 + LANES) for t0 in range(0, tt, LANES)]
        prev = None
        for ls in lanes + [None]:
            ss = [sorted_lists(s1[:, ls]), sorted_lists(s2[:, ls])] if ls is not None else None
            if prev is not None:
                cands, max1, max2 = candidates(prev, vb)
                tops = []
            for k in range(PEER_DEPTH):
                if ss is not None:
                    ss = value_step(k, ss, ls, vb)
                if prev is not None:
                    cands = cand_step(k, cands, tops)
            if prev is not None:
                finish(hh, tops, max1, max2, s1[:, prev], s2[:, prev], prev)
            prev = ls

    v_scr[:, PEER_TOPK:, :] = jnp.full((4, PEER_VROWS - PEER_TOPK, tt), NEG_INF, F32)
    lax.fori_loop(0, PEER_HEADS // 2, head_pair, 0)


def peer_select(x, gain, wq_t, subkeys, *, tt=512):
    n, d = x.shape
    tt = min(tt, n)
    assert n % tt == 0
    assert tt % LANES == 0
    tab = pl.BlockSpec((tt // LANES * PEER_ROWS, LANES), lambda i: (i, 0))
    tab_shape = jax.ShapeDtypeStruct((n // LANES * PEER_ROWS, LANES), F32)
    kern = functools.partial(_peer_select_kernel, tt=tt)
    return pl.pallas_call(
        kern, name="peer_select", grid=(n // tt,),
        in_specs=[pl.BlockSpec((tt, d), lambda i: (i, 0)),
                  pl.BlockSpec((1, d), lambda i: (0, 0)),
                  pl.BlockSpec(wq_t.shape, lambda i: (0, 0)),
                  pl.BlockSpec(subkeys.shape, lambda i: (0, 0, 0))],
        out_specs=[pl.BlockSpec((d, tt), lambda i: (0, i)), tab, tab, tab, tab],
        out_shape=[jax.ShapeDtypeStruct((d, n), BF16)] + [tab_shape] * 4,
        scratch_shapes=[pltpu.VMEM((2 * PEER_ROWS, tt), F32), pltpu.VMEM((4, PEER_VROWS, tt), F32)],
        compiler_params=_params(("parallel",)))(x, gain.reshape(1, d), wq_t, subkeys)


PEER_CHUNK = 512
PEER_PRE_K = MXU_WIDTH
PEER_VAL_K = MXU_WIDTH
PEER_DENSE_FLAGS = None


def _peer_dense_kernel(x_ref, ht_ref, d_ref, e1_ref, s2_ref, e2_ref, u0_ref, un_ref, vtp_ref, vtl_ref, o_ref,
                       acc_scr, a0_scr, a1_scr, p0_scr, p1_scr, *, tt):
    a_scr = (a0_scr, a1_scr)
    p_scr = (p0_scr, p1_scr)
    c = pl.program_id(1)
    last = pl.num_programs(1) - 1
    rows_per_chunk = PEER_CHUNK // PEER_NKEYS
    groups = PEER_NKEYS // SUBLANES

    @pl.when(c == 0)
    def _():
        acc_scr[...] = jnp.zeros(acc_scr.shape, F32)
        p_scr[1][...] = jnp.zeros(p_scr[1].shape, BF16)
        a_scr[0][...] = _dot(u0_ref[...], ht_ref[...])

    def key_rows(r, lg):
        i = c * rows_per_chunk + r
        base = lg * PEER_ROWS + i
        d_rows = [d_ref[pl.ds(base + hh * PEER_NKEYS, SUBLANES, stride=0), :] for hh in range(PEER_HEADS)]
        e_rows = [e1_ref[pl.ds(base + hh * PEER_NKEYS, SUBLANES, stride=0), :] for hh in range(PEER_HEADS)]
        return d_rows, e_rows

    def weights_item(r, g2, lg, cur, rows):
        d_rows, e_rows = rows
        ls = slice(lg * LANES, (lg + 1) * LANES)
        vals = []
        for g in (g2, g2 + 1):
            w = None
            for hh in range(PEER_HEADS):
                lo = lg * PEER_ROWS + hh * PEER_NKEYS + g * SUBLANES
                js = slice(lo, lo + SUBLANES)
                sel = jnp.where(s2_ref[js, :] >= d_rows[hh], e_rows[hh] * e2_ref[js, :], 0.0)
                w = sel if w is None else w + sel
            a = a_scr[cur][r * PEER_NKEYS + g * SUBLANES:r * PEER_NKEYS + (g + 1) * SUBLANES, ls]
            vals.append(_gelu(a) * w)
        lo = r * PEER_NKEYS + g2 * SUBLANES
        p_scr[cur][lo:lo + 2 * SUBLANES, ls] = jnp.concatenate(vals, axis=0).astype(BF16)

    def preact_piece(k, ls, nxt):
        ks = slice(k * PEER_PRE_K, (k + 1) * PEER_PRE_K)
        part = _dot(un_ref[:, ks], ht_ref[ks, ls])
        if k == 0:
            a_scr[nxt][:, ls] = part
        else:
            a_scr[nxt][:, ls] += part

    def value_piece(k, ls, nxt):
        ks = slice(k * PEER_VAL_K, (k + 1) * PEER_VAL_K)
        acc_scr[:, ls] += _dot(vtp_ref[0, :, ks], p_scr[nxt][ks, ls])

    def step(cur, nxt):
        n_pre = ht_ref.shape[0] // PEER_PRE_K
        n_val = PEER_CHUNK // PEER_VAL_K
        merged = sorted([((k + 0.5) / n_pre, 0, preact_piece, k) for k in range(n_pre)]
                        + [((k + 0.5) / n_val, 1, value_piece, k) for k in range(n_val)],
                        key=lambda e: e[:2])
        pieces = {r: [] for r in range(rows_per_chunk)}
        for idx, (_, _, fn, k) in enumerate(merged):
            pieces[idx * rows_per_chunk // len(merged)].append((fn, k))
        for t0 in range(0, tt, MXU_WIDTH):
            ls = slice(t0, t0 + MXU_WIDTH)
            for r in range(rows_per_chunk):
                for fn, k in pieces[r]:
                    fn(k, ls, nxt)
                for lg in range(t0 // LANES, (t0 + MXU_WIDTH) // LANES):
                    rows = key_rows(r, lg)
                    for g2 in range(0, groups, 2):
                        weights_item(r, g2, lg, cur, rows)

    @pl.when((c & 1) == 0)
    def _():
        step(0, 1)

    @pl.when((c & 1) == 1)
    def _():
        step(1, 0)

    @pl.when(c == last)
    def _():
        acc = acc_scr[...] + _dot(vtl_ref[0], p_scr[1][...])
        o_ref[...] = x_ref[...] + acc.T


def peer_dense(x, ht, d_thr, e1, s2, e2, u_tab, v_tab_t, *, tt=512):
    n, d = x.shape
    tt = min(tt, n)
    assert n % tt == 0 and tt % MXU_WIDTH == 0
    nchunks = PEER_EXPERTS // PEER_CHUNK
    assert nchunks % 2 == 0
    tab = pl.BlockSpec((tt // LANES * PEER_ROWS, LANES), lambda i, c: (i, 0))
    return pl.pallas_call(
        functools.partial(_peer_dense_kernel, tt=tt), name="peer_dense",
        grid=(n // tt, nchunks),
        in_specs=[pl.BlockSpec((tt, d), lambda i, c: (i, 0)),
                  pl.BlockSpec((d, tt), lambda i, c: (0, i)), tab, tab, tab, tab,
                  pl.BlockSpec((PEER_CHUNK, d), lambda i, c: (0, 0)),
                  pl.BlockSpec((PEER_CHUNK, d), lambda i, c: (jnp.minimum(c + 1, nchunks - 1), 0)),
                  pl.BlockSpec((1, d, PEER_CHUNK), lambda i, c: (jnp.maximum(c - 1, 0), 0, 0)),
                  pl.BlockSpec((1, d, PEER_CHUNK), lambda i, c: (nchunks - 1, 0, 0))],
        out_specs=pl.BlockSpec((tt, d), lambda i, c: (i, 0)),
        out_shape=jax.ShapeDtypeStruct((n, d), F32),
        scratch_shapes=[pltpu.VMEM((d, tt), F32)] + [pltpu.VMEM((PEER_CHUNK, tt), F32)] * 2
        + [pltpu.VMEM((PEER_CHUNK, tt), BF16)] * 2,
        compiler_params=_params(("parallel", "arbitrary"), flags=PEER_DENSE_FLAGS))(
            x, ht, d_thr, e1, s2, e2, u_tab, u_tab, v_tab_t, v_tab_t)


def _norm_kernel(x_ref, g_ref, o_ref):
    o_ref[...] = _rms(x_ref[...], g_ref[...])


def final_norm(x, gain, *, tn=512):
    n, d = x.shape
    tn = min(tn, n)
    assert n % tn == 0
    tile = pl.BlockSpec((tn, d), lambda i: (i, 0))
    return pl.pallas_call(
        _norm_kernel, name="final_norm", grid=(n // tn,),
        in_specs=[tile, pl.BlockSpec((1, d), lambda i: (0, 0))], out_specs=tile,
        out_shape=jax.ShapeDtypeStruct((n, d), F32),
        compiler_params=_params(("parallel",)))(x, gain.reshape(1, d))


def _trunk(x, pos0, mem_k, mem_v, pool_past, conv_past, cache_k, cache_v, page_table, W):
    b, t, d = x.shape
    n = b * t
    x = x.reshape(n, d)
    new_k, new_v, new_pool, new_conv = [], [], [], []
    for l in range(DEPTH):
        i = l // 2
        if l % 2 == 0:
            lam_init = 0.8 - 0.6 * math.exp(-0.3 * l)
            q, k, v, p = fused_linear([x], [W["w_in_ab"][i]], name="mix_in_ab", gain=W["norm_mix"][l],
                                      splits=[A_QK] * 4)
            q3, k3, v3, p3 = (a.reshape(b, t, A_QK) for a in (q, k, v, p))
            lam_vecs = [W[nm][i] for nm in ("lambda_q1", "lambda_k1", "lambda_q2", "lambda_k2")]
            if page_table is None:
                o = dattn_prompt(q3, k3, v3, lam_vecs, W["subln_g"][i], lam_init)
            else:
                o = dattn_sample(q3, k3, v3, cache_k, cache_v, i, page_table, lam_vecs,
                                 W["subln_g"][i], lam_init)
            past = pool_past[i]
            past16 = jnp.pad(past, ((0, 0), (1, 0), (0, 0)))
            pooled = pool_mix(p3, past16, pos0, W["pool_w"][i], W["pool_scale"][i])
            x = fused_linear([o.reshape(n, A_QK), pooled.reshape(n, POOL_DIM)],
                             [W["w_out_ab"][i][:A_QK], W["w_out_ab"][i][A_QK:]], name="mix_out_ab", res=x)
            new_k.append(k3.reshape(b, t, A_HEADS, A_HD))
            new_v.append(v3.reshape(b, t, A_HEADS, A_HD))
            new_pool.append(jnp.concatenate([past, p3], axis=1)[:, -(POOL_MAX - 1):])
        else:
            bg, u = fused_linear([x], [W["w_in_c"][i]], name="mix_in_c", gain=W["norm_mix"][l], gate=True)
            u3 = u.reshape(b, t, d)
            past = conv_past[i]
            past8 = jnp.pad(past, ((0, 0), (CONV_HALO - (CONV_W - 1), 0), (0, 0)))
            z = short_conv(u3, bg.reshape(b, t, d), past8, W["conv_w"][i])
            x = fused_linear([z.reshape(n, d)], [W["w_out_c"][i]], name="mix_out_c", res=x)
            new_conv.append(jnp.concatenate([past, u3], axis=1)[:, -(CONV_W - 1):])
        qm = fused_linear([x], [W["w_mq"][l]], name="mem_q", gain=W["norm_mem"][l])
        if page_table is None:
            om = mem_attn(qm.reshape(b, t, d), mem_k, mem_v, l)
        else:
            om = mem_attn_rows(qm.reshape(b, t, d), mem_k, mem_v, l)
        x = fused_linear([om.reshape(n, d)], [W["w_mo"][l]], name="mem_o", res=x)
        sel = peer_select(x, W["norm_ffn"][l], W["peer_wq_t"][l], W["peer_subkeys"][l])
        x = peer_dense(x, *sel, W["peer_u"][l], W["peer_v_t"][l])
    y = final_norm(x, W["norm_final"]).reshape(b, t, d)
    return y, jnp.stack(new_k), jnp.stack(new_v), jnp.stack(new_pool), jnp.stack(new_conv)


def _bf(w):
    return w.astype(BF16)


def _prepare_weights(**w):
    out = dict(w)
    for name in ("w_in_ab", "w_out_ab", "pool_w", "w_in_c", "w_out_c", "w_mq", "w_mo", "peer_u"):
        out[name] = _bf(w[name])
    out["peer_wq_t"] = _bf(jnp.swapaxes(w["peer_wq"], 1, 2))
    out["peer_subkeys"] = _bf(w["peer_subkeys"].reshape(DEPTH, PEER_HEADS * 2, PEER_NKEYS, PEER_NKEYS))
    depth, _, d = w["peer_v"].shape
    v_chunks = w["peer_v"].reshape(depth, PEER_EXPERTS // PEER_CHUNK, PEER_CHUNK, d)
    out["peer_v_t"] = _bf(jnp.swapaxes(v_chunks, 2, 3))
    return out


def kernel(x_prompt, x_sample, cache_attn_k, cache_attn_v, state_pool, state_conv, cache_mem_k, cache_mem_v, page_table, mem_prompt, norm_mix, norm_mem, norm_ffn, norm_final, w_in_ab, w_out_ab, lambda_q1, lambda_k1, lambda_q2, lambda_k2, subln_g, pool_w, pool_scale, w_in_c, conv_w, w_out_c, w_mq, w_mk, w_mv, w_mo, peer_wq, peer_subkeys, peer_u, peer_v):
    W = _prepare_weights(
        norm_mix=norm_mix, norm_mem=norm_mem, norm_ffn=norm_ffn, norm_final=norm_final, w_in_ab=w_in_ab,
        w_out_ab=w_out_ab, lambda_q1=lambda_q1, lambda_k1=lambda_k1, lambda_q2=lambda_q2, lambda_k2=lambda_k2,
        subln_g=subln_g, pool_w=pool_w, pool_scale=pool_scale, w_in_c=w_in_c, conv_w=conv_w, w_out_c=w_out_c,
        w_mq=w_mq, w_mo=w_mo, peer_wq=peer_wq, peer_subkeys=peer_subkeys, peer_u=peer_u, peer_v=peer_v)
    bp, sp, d = x_prompt.shape
    mk_p, mv_p = mem_project(mem_prompt.reshape(bp * N_MEM, d), _bf(w_mk), _bf(w_mv))
    mk_p = mk_p.reshape(DEPTH, bp, N_MEM, d)
    mv_p = mv_p.reshape(DEPTH, bp, N_MEM, d)
    pool0 = jnp.zeros((w_in_ab.shape[0], bp, POOL_MAX - 1, POOL_DIM), F32)
    conv0 = jnp.zeros((w_in_c.shape[0], bp, CONV_W - 1, d), F32)
    y_p, k_p, v_p, pool_p, conv_p = _trunk(x_prompt, 0, mk_p, mv_p, pool0, conv0, None, None, None, W)
    past_len = page_table.shape[1] * cache_attn_k.shape[2]
    n_layers, n_pool = cache_attn_k.shape[:2]
    ck = cache_attn_k.reshape(n_layers, n_pool, PAGE_SIZE * A_HEADS, A_HD)
    cv = cache_attn_v.reshape(n_layers, n_pool, PAGE_SIZE * A_HEADS, A_HD)
    y_s, k_s, v_s, pool_s, conv_s = _trunk(
        x_sample, past_len, _mem_rows_view(cache_mem_k), _mem_rows_view(cache_mem_v), state_pool, state_conv,
        ck, cv, page_table, W)
    mshape = (DEPTH, bp, N_MEM, MEM_HEADS, MEM_DH)
    return (y_p, y_s, k_p, v_p, pool_p, conv_p, mk_p.reshape(mshape), mv_p.reshape(mshape),
            k_s, v_s, pool_s, conv_s)
```

```python
import functools
import math

import jax
import jax.numpy as jnp
from jax import lax
from jax.experimental import pallas as pl
from jax.experimental.pallas import tpu as pltpu

F32 = jnp.float32
BF16 = jnp.bfloat16

D_MODEL = 1024
DEPTH = 4
EPS = 1e-5
A_HEADS = 4
A_DH = 64
A_HD = 2 * A_DH
A_QK = A_HEADS * A_HD
POOL_WINDOWS = (2, 4, 8, 16)
POOL_GDIM = 128
POOL_DIM = 512
POOL_MAX = 16
CONV_W = 3
N_MEM = 256
MEM_HEADS = 4
MEM_DH = 256
PEER_HEADS = 8
PEER_NKEYS = 128
PEER_EXPERTS = PEER_NKEYS * PEER_NKEYS
PEER_TOPK = 16
PAGE_SIZE = 128

SUBLANES = 8
LANES = 128
MXU_WIDTH = 256
VMEM_LIMIT = 56 * 1024 * 1024
NEG_INF = float("-inf")
MASK_VALUE = float(jnp.finfo(jnp.float32).min)
LOG2E = 1.4426950408889634


def _params(sem, vmem=VMEM_LIMIT, flags=None):
    return pltpu.CompilerParams(dimension_semantics=sem, vmem_limit_bytes=vmem, flags=flags)


def _rms(x, g):
    return x * lax.rsqrt(jnp.mean(x * x, axis=-1, keepdims=True) + EPS) * g


def _gelu(a):
    return 0.5 * a * (1.0 + lax.erf(a * (2.0 ** -0.5)))


def _dot(a, b):
    return jnp.dot(a, b, preferred_element_type=F32)


def _dot_nt(a, b):
    return lax.dot_general(a, b, (((1,), (1,)), ((), ())), preferred_element_type=F32)


def _linear_kernel(*refs, n_in, has_norm, has_res, splits, gate):
    a_refs = refs[:n_in]
    w_refs = refs[n_in:2 * n_in]
    idx = 2 * n_in
    g_ref = refs[idx] if has_norm else None
    idx += int(has_norm)
    r_ref = refs[idx] if has_res else None
    idx += int(has_res)
    out_refs = refs[idx:]
    acc = None
    for a_ref, w_ref in zip(a_refs, w_refs):
        a = a_ref[...]
        if has_norm:
            a = _rms(a, g_ref[...])
        d = _dot(a.astype(BF16), w_ref[...])
        acc = d if acc is None else acc + d
    if has_res:
        acc = acc + r_ref[...]
    if gate:
        d3 = acc.shape[1] // 3
        out_refs[0][...] = acc[:, :d3]
        out_refs[1][...] = acc[:, d3:2 * d3] * acc[:, 2 * d3:]
    else:
        off = 0
        for o_ref, s in zip(out_refs, splits):
            o_ref[...] = acc[:, off:off + s]
            off += s


def fused_linear(a_list, w_list, *, name, gain=None, res=None, splits=None, gate=False, tn=512):
    n = a_list[0].shape[0]
    m = w_list[0].shape[1]
    tn = min(tn, n)
    assert n % tn == 0
    if gate:
        out_cols = [m // 3, m // 3]
    else:
        out_cols = list(splits) if splits is not None else [m]
        assert sum(out_cols) == m
    in_specs, args = [], []
    for a in a_list:
        in_specs.append(pl.BlockSpec((tn, a.shape[1]), lambda i: (i, 0)))
        args.append(a)
    for w in w_list:
        in_specs.append(pl.BlockSpec(w.shape, lambda i: (0, 0)))
        args.append(w)
    if gain is not None:
        in_specs.append(pl.BlockSpec((1, gain.shape[-1]), lambda i: (0, 0)))
        args.append(gain.reshape(1, -1))
    if res is not None:
        in_specs.append(pl.BlockSpec((tn, m), lambda i: (i, 0)))
        args.append(res)
    out_shape = [jax.ShapeDtypeStruct((n, c), F32) for c in out_cols]
    out_specs = [pl.BlockSpec((tn, c), lambda i: (i, 0)) for c in out_cols]
    kern = functools.partial(_linear_kernel, n_in=len(a_list), has_norm=gain is not None,
                             has_res=res is not None, splits=tuple(out_cols), gate=gate)
    outs = pl.pallas_call(
        kern, name=name, grid=(n // tn,), in_specs=in_specs, out_specs=out_specs, out_shape=out_shape,
        compiler_params=_params(("parallel",)))(*args)
    return outs if len(outs) > 1 else outs[0]


def _memproj_kernel(a_ref, wk_ref, wv_ref, ok_ref, ov_ref):
    a = a_ref[...].astype(BF16)
    ok_ref[0] = _dot(a, wk_ref[0])
    ov_ref[0] = _dot(a, wv_ref[0])


def mem_project(mem2d, wk, wv):
    n, d = mem2d.shape
    depth = wk.shape[0]
    spec_w = pl.BlockSpec((1, d, d), lambda l: (l, 0, 0))
    spec_o = pl.BlockSpec((1, n, d), lambda l: (l, 0, 0))
    return pl.pallas_call(
        _memproj_kernel, name="mem_project", grid=(depth,),
        in_specs=[pl.BlockSpec((n, d), lambda l: (0, 0)), spec_w, spec_w],
        out_specs=[spec_o, spec_o],
        out_shape=[jax.ShapeDtypeStruct((depth, n, d), F32)] * 2,
        compiler_params=_params(("parallel",)))(mem2d, wk, wv)


def _lambda_value(lq1, lk1, lq2, lk2, lam_init):
    return (jnp.exp(jnp.sum(lq1[...] * lk1[...], axis=-1, keepdims=True))
            - jnp.exp(jnp.sum(lq2[...] * lk2[...], axis=-1, keepdims=True)) + lam_init)


def _subln(o, g, lam_init):
    return _rms(o, g) * (1.0 - lam_init)


def _dattn_prompt_kernel(lq1, lk1, lq2, lk2, g_ref, q_ref, k_ref, v_ref, o_ref, kb_scr, vt_scr, *,
                         tq, lam_init):
    qi = pl.program_id(2)
    nblk = kb_scr.shape[0]

    @pl.when(qi == 0)
    def _():
        for jb in range(nblk):
            kb_scr[jb] = k_ref[0, jb * tq:(jb + 1) * tq, :].astype(BF16)
            vt_scr[jb] = v_ref[0, jb * tq:(jb + 1) * tq, :].T.astype(BF16)

    qt = (q_ref[0] * (A_DH ** -0.5 * LOG2E)).T
    sub = lax.broadcasted_iota(jnp.int32, (A_HD, tq), 0)
    qz = (jnp.where(sub < A_DH, qt, 0.0).astype(BF16), jnp.where(sub >= A_DH, qt, 0.0).astype(BF16))

    def blocks(js, carry, masked):
        kbs = [kb_scr[j] for j in js]
        vts = [vt_scr[j] for j in js]
        out = []
        scores = [[_dot(kb, qz[mi]) for kb in kbs] for mi in range(2)]
        for mi in range(2):
            m, l, a = carry[3 * mi:3 * mi + 3]
            ss = scores[mi]
            if masked:
                kk = lax.broadcasted_iota(jnp.int32, (tq, tq), 0)
                qq = lax.broadcasted_iota(jnp.int32, (tq, tq), 1)
                ss = [jnp.where(kk <= qq, s, MASK_VALUE) for s in ss]
            m_new = functools.reduce(jnp.maximum, [m] + [jnp.max(s, axis=0, keepdims=True) for s in ss])
            alpha = jnp.exp2(m - m_new)
            l = alpha * l
            a = alpha * a
            for s, vt in zip(ss, vts):
                p = jnp.exp2(s - m_new)
                l = l + jnp.sum(p, axis=0, keepdims=True)
                a = a + _dot(vt, p.astype(BF16))
            out += [m_new, l, a]
        return tuple(out)

    init = (jnp.full((1, tq), NEG_INF, F32), jnp.zeros((1, tq), F32), jnp.zeros((A_HD, tq), F32)) * 2
    carry = lax.fori_loop(0, qi >> 1, lambda j, c: blocks((2 * j, 2 * j + 1), c, False), init)
    carry = lax.fori_loop(0, qi & 1, lambda j, c: blocks((qi - 1,), c, False), carry)
    m1, l1, a1, m2, l2, a2 = blocks((qi,), carry, True)
    lam = _lambda_value(lq1, lk1, lq2, lk2, lam_init)
    ot = a1 / l1 - lam * (a2 / l2)
    o_ref[0] = _subln(ot.T, g_ref[...], lam_init)


def dattn_prompt(q, k, v, lam_vecs, subln_g, lam_init, *, tq=512):
    b, t, _ = q.shape
    tq = min(tq, t)
    assert t % tq == 0
    vec = pl.BlockSpec((1, A_DH), lambda bi, h, qi: (0, 0))
    kern = functools.partial(_dattn_prompt_kernel, tq=tq, lam_init=lam_init)
    return pl.pallas_call(
        kern, name="dattn_prompt", grid=(b, A_HEADS, t // tq),
        in_specs=[vec, vec, vec, vec,
                  pl.BlockSpec((1, A_HD), lambda bi, h, qi: (0, 0)),
                  pl.BlockSpec((1, tq, A_HD), lambda bi, h, qi: (bi, qi, h)),
                  pl.BlockSpec((1, t, A_HD), lambda bi, h, qi: (bi, 0, h)),
                  pl.BlockSpec((1, t, A_HD), lambda bi, h, qi: (bi, 0, h))],
        out_specs=pl.BlockSpec((1, tq, A_HD), lambda bi, h, qi: (bi, qi, h)),
        out_shape=jax.ShapeDtypeStruct((b, t, A_QK), F32),
        scratch_shapes=[pltpu.VMEM((t // tq, tq, A_HD), BF16), pltpu.VMEM((t // tq, A_HD, tq), BF16)],
        compiler_params=_params(("parallel", "parallel", "arbitrary")))(
            *[x.reshape(1, A_DH) for x in lam_vecs], subln_g.reshape(1, A_HD), q, k, v)


def _dattn_sample_kernel(pt_ref, lq1, lk1, lq2, lk2, g_ref, q_ref, kn_ref, vn_ref, *rest,
                         n_pages, t_new, lam_init):
    k_refs = rest[:n_pages]
    v_refs = rest[n_pages:2 * n_pages]
    o_ref = rest[2 * n_pages]
    rows = A_HEADS * 2 * t_new
    hbits = A_HEADS.bit_length() - 1
    tbits = t_new.bit_length() - 1
    q = q_ref[0] * (A_DH ** -0.5)
    lane = lax.broadcasted_iota(jnp.int32, (t_new, A_HD), 1)
    parts = []
    for h in range(A_HEADS):
        qh = q[:, h * A_HD:(h + 1) * A_HD]
        parts += [jnp.where(lane < A_DH, qh, 0.0), jnp.where(lane >= A_DH, qh, 0.0)]
    qz = jnp.concatenate(parts, axis=0).astype(BF16)

    def bias(ncols, causal):
        rr = lax.broadcasted_iota(jnp.int32, (rows, ncols), 0)
        cc = lax.broadcasted_iota(jnp.int32, (rows, ncols), 1)
        ok = (cc & (A_HEADS - 1)) == (rr >> (tbits + 1))
        if causal:
            ok = ok & ((cc >> hbits) <= (rr & (t_new - 1)))
        return jnp.where(ok, 0.0, NEG_INF)

    page_bias = bias(PAGE_SIZE * A_HEADS, False)
    s_pages = [_dot_nt(qz, k_ref[0, 0].astype(BF16)) + page_bias for k_ref in k_refs]
    s_new = _dot_nt(qz, kn_ref[0].astype(BF16)) + bias(t_new * A_HEADS, True)
    m = jnp.max(functools.reduce(jnp.maximum, s_pages), axis=-1, keepdims=True)
    m = jnp.maximum(m, jnp.max(s_new, axis=-1, keepdims=True))
    p_new = jnp.exp(s_new - m)
    l = jnp.sum(p_new, axis=-1, keepdims=True)
    acc = _dot(p_new.astype(BF16), vn_ref[0].astype(BF16))
    for s, v_ref in zip(s_pages, v_refs):
        p = jnp.exp(s - m)
        l = l + jnp.sum(p, axis=-1, keepdims=True)
        acc = acc + _dot(p.astype(BF16), v_ref[0, 0].astype(BF16))
    acc = acc / l
    lam = _lambda_value(lq1, lk1, lq2, lk2, lam_init)
    for h in range(A_HEADS):
        r0 = h * 2 * t_new
        o = acc[r0:r0 + t_new] - lam * acc[r0 + t_new:r0 + 2 * t_new]
        o_ref[0, :, h * A_HD:(h + 1) * A_HD] = _subln(o, g_ref[...], lam_init)


def dattn_sample(q, k_new, v_new, cache_k, cache_v, layer, page_table, lam_vecs, subln_g, lam_init):
    b, t_new, _ = q.shape
    assert t_new & (t_new - 1) == 0
    n_pages = page_table.shape[1]
    pt = page_table.reshape(-1)
    vec = pl.BlockSpec((1, A_DH), lambda bi, pt: (0, 0))
    qspec = pl.BlockSpec((1, t_new, A_QK), lambda bi, pt: (bi, 0, 0))
    nspec = pl.BlockSpec((1, t_new * A_HEADS, A_HD), lambda bi, pt: (bi, 0, 0))

    def page(r):
        return pl.BlockSpec((1, 1, PAGE_SIZE * A_HEADS, A_HD),
                            lambda bi, pt: (layer, pt[bi * n_pages + r], 0, 0))

    pages = [page(r) for r in range(n_pages)]
    kern = functools.partial(_dattn_sample_kernel, n_pages=n_pages, t_new=t_new, lam_init=lam_init)
    grid_spec = pltpu.PrefetchScalarGridSpec(
        num_scalar_prefetch=1, grid=(b,),
        in_specs=[vec, vec, vec, vec, pl.BlockSpec((1, A_HD), lambda bi, pt: (0, 0)),
                  qspec, nspec, nspec] + pages + pages,
        out_specs=qspec)
    return pl.pallas_call(
        kern, name="dattn_sample", grid_spec=grid_spec,
        out_shape=jax.ShapeDtypeStruct((b, t_new, A_QK), F32),
        compiler_params=_params(("parallel",)))(
            pt, *[x.reshape(1, A_DH) for x in lam_vecs], subln_g.reshape(1, A_HD), q,
            k_new.reshape(b, t_new * A_HEADS, A_HD), v_new.reshape(b, t_new * A_HEADS, A_HD),
            *([cache_k] * n_pages), *([cache_v] * n_pages))


def _pool_kernel(p_ref, halo_ref, past_ref, w_ref, sc_ref, o_ref, scr, *, tt, pos0):
    ti = pl.program_id(1)
    scr[0:POOL_MAX, :] = jnp.where(ti == 0, past_ref[0], halo_ref[0])
    scr[POOL_MAX:, :] = p_ref[0]
    pos = pos0 + ti * tt + lax.broadcasted_iota(jnp.int32, (tt, POOL_GDIM), 0)
    for g, w in enumerate(POOL_WINDOWS):
        sl = slice(g * POOL_GDIM, (g + 1) * POOL_GDIM)
        x = scr[POOL_MAX:, sl]
        s = x
        for jj in range(1, w):
            s = s + scr[POOL_MAX - jj:POOL_MAX - jj + tt, sl]
        cnt = jnp.minimum(w, pos + 1).astype(F32)
        d = s / cnt - x
        y = _dot(d.astype(BF16), w_ref[g])
        o_ref[0, :, sl] = y * sc_ref[:, sl]


def pool_mix(p, past16, pos0, pool_w, pool_scale, *, tt=1024):
    b, t, _ = p.shape
    tt = min(tt, t)
    halo_src = p if t >= POOL_MAX else past16
    hb = tt // POOL_MAX
    kern = functools.partial(_pool_kernel, tt=tt, pos0=pos0)
    return pl.pallas_call(
        kern, name="pool_mix", grid=(b, t // tt),
        in_specs=[pl.BlockSpec((1, tt, POOL_DIM), lambda bi, ti: (bi, ti, 0)),
                  pl.BlockSpec((1, POOL_MAX, POOL_DIM), lambda bi, ti: (bi, jnp.maximum(ti * hb - 1, 0), 0)),
                  pl.BlockSpec((1, POOL_MAX, POOL_DIM), lambda bi, ti: (bi, 0, 0)),
                  pl.BlockSpec(pool_w.shape, lambda bi, ti: (0, 0, 0)),
                  pl.BlockSpec((1, POOL_DIM), lambda bi, ti: (0, 0))],
        out_specs=pl.BlockSpec((1, tt, POOL_DIM), lambda bi, ti: (bi, ti, 0)),
        out_shape=jax.ShapeDtypeStruct((b, t, POOL_DIM), F32),
        scratch_shapes=[pltpu.VMEM((POOL_MAX + tt, POOL_DIM), F32)],
        compiler_params=_params(("parallel", "arbitrary")))(
            p, halo_src, past16, pool_w, pool_scale.reshape(1, POOL_DIM))


CONV_HALO = 8


def _conv_kernel(u_ref, halo_ref, past_ref, b_ref, cw_ref, w_ref, x_ref, o_ref, scr, *, tt):
    ti = pl.program_id(1)
    scr[0:CONV_HALO, :] = jnp.where(ti == 0, past_ref[0], halo_ref[0])
    scr[CONV_HALO:, :] = u_ref[0]
    y = None
    for jj in range(CONV_W):
        off = CONV_HALO - (CONV_W - 1) + jj
        term = cw_ref[jj:jj + 1, :] * scr[off:off + tt, :]
        y = term if y is None else y + term
    o_ref[0] = x_ref[0] + _dot((b_ref[0] * y).astype(BF16), w_ref[...])


def short_conv(u, bgate, past8, conv_w, w_out, x, *, tt=512):
    b, t, d = u.shape
    tt = min(tt, t)
    hb = tt // CONV_HALO
    tile = pl.BlockSpec((1, tt, d), lambda bi, ti: (bi, ti, 0))
    kern = functools.partial(_conv_kernel, tt=tt)
    return pl.pallas_call(
        kern, name="short_conv", grid=(b, t // tt),
        in_specs=[tile,
                  pl.BlockSpec((1, CONV_HALO, d), lambda bi, ti: (bi, jnp.maximum(ti * hb - 1, 0), 0)),
                  pl.BlockSpec((1, CONV_HALO, d), lambda bi, ti: (bi, 0, 0)),
                  tile,
                  pl.BlockSpec((CONV_W, d), lambda bi, ti: (0, 0)),
                  pl.BlockSpec(w_out.shape, lambda bi, ti: (0, 0)),
                  tile],
        out_specs=tile,
        out_shape=jax.ShapeDtypeStruct((b, t, d), F32),
        scratch_shapes=[pltpu.VMEM((CONV_HALO + tt, d), F32)],
        compiler_params=_params(("parallel", "arbitrary")))(u, u, past8, bgate, conv_w, w_out, x)


def _memblock_kernel(x_ref, g_ref, wq_ref, k_ref, v_ref, wo_ref, o_ref):
    x = x_ref[0]
    q = _dot(_rms(x, g_ref[...]).astype(BF16), wq_ref[...]) * (MEM_DH ** -0.5)
    outs = []
    for h in range(MEM_HEADS):
        sl = slice(h * MEM_DH, (h + 1) * MEM_DH)
        s = _dot_nt(q[:, sl].astype(BF16), k_ref[0, 0, :, sl].astype(BF16))
        s = s - jnp.max(s, axis=-1, keepdims=True)
        p = jnp.exp(s)
        p = p / jnp.sum(p, axis=-1, keepdims=True)
        outs.append(_dot(p.astype(BF16), v_ref[0, 0, :, sl].astype(BF16)).astype(BF16))
    o_ref[0] = x + _dot(jnp.concatenate(outs, axis=1), wo_ref[...])


def mem_block(x, gain, wq, mk, mv, wo, layer, *, tq=512):
    b, t, d = x.shape
    tq = min(tq, t)
    tile = pl.BlockSpec((1, tq, d), lambda bi, ti: (bi, ti, 0))
    mem = pl.BlockSpec((1, 1, N_MEM, d), lambda bi, ti: (layer, bi, 0, 0))
    wspec = pl.BlockSpec((d, d), lambda bi, ti: (0, 0))
    return pl.pallas_call(
        _memblock_kernel, name="mem_block", grid=(b, t // tq),
        in_specs=[tile, pl.BlockSpec((1, d), lambda bi, ti: (0, 0)), wspec, mem, mem, wspec],
        out_specs=tile, out_shape=jax.ShapeDtypeStruct((b, t, d), F32),
        compiler_params=_params(("parallel", "parallel")))(x, gain.reshape(1, d), wq, mk, mv, wo)


MEM_SPLIT = MEM_DH // LANES
MEM_ROWS = MEM_HEADS * MEM_SPLIT


def _memattn_rows_kernel(q_ref, k_ref, v_ref, o_ref, *, t):
    ncol = N_MEM * MEM_ROWS
    q = q_ref[0] * (MEM_DH ** -0.5)
    qz = jnp.concatenate([q[:, kk * LANES:(kk + 1) * LANES] for kk in range(MEM_ROWS)], axis=0)
    sfull = _dot_nt(qz.astype(BF16), k_ref[0, 0].astype(BF16))
    sfull = sfull.reshape(MEM_HEADS, MEM_SPLIT, t, ncol)
    s = sfull[:, 0]
    for j in range(1, MEM_SPLIT):
        part = sfull[:, j].reshape(MEM_HEADS * t, ncol)
        s = s + pltpu.roll(part, ncol - j * MEM_HEADS, axis=1).reshape(MEM_HEADS, t, ncol)
    s = s.reshape(MEM_HEADS * t, ncol)
    rr = lax.broadcasted_iota(jnp.int32, (MEM_HEADS * t, ncol), 0)
    cc = lax.broadcasted_iota(jnp.int32, (MEM_HEADS * t, ncol), 1)
    valid = (cc & (MEM_ROWS - 1)) == (rr >> (t.bit_length() - 1))
    s = jnp.where(valid, s, NEG_INF)
    s = s - jnp.max(s, axis=-1, keepdims=True)
    p = jnp.exp(s)
    p = p / jnp.sum(p, axis=-1, keepdims=True)
    parts = [p.reshape(MEM_HEADS, t, ncol)]
    for j in range(1, MEM_SPLIT):
        parts.append(pltpu.roll(p, j * MEM_HEADS, axis=1).reshape(MEM_HEADS, t, ncol))
    pz = jnp.stack(parts, axis=1).reshape(MEM_ROWS * t, ncol)
    o = _dot(pz.astype(BF16), v_ref[0, 0].astype(BF16))
    for kk in range(MEM_ROWS):
        o_ref[0, :, kk * LANES:(kk + 1) * LANES] = o[kk * t:(kk + 1) * t]


def mem_attn_rows(q, mk_rows, mv_rows, layer):
    b, t, d = q.shape
    assert t & (t - 1) == 0 and t % SUBLANES == 0
    tile = pl.BlockSpec((1, t, d), lambda bi: (bi, 0, 0))
    mem = pl.BlockSpec((1, 1, N_MEM * MEM_ROWS, LANES), lambda bi: (layer, bi, 0, 0))
    return pl.pallas_call(
        functools.partial(_memattn_rows_kernel, t=t), name="mem_attn_rows", grid=(b,),
        in_specs=[tile, mem, mem], out_specs=tile,
        out_shape=jax.ShapeDtypeStruct((b, t, d), F32),
        compiler_params=_params(("parallel",)))(q, mk_rows, mv_rows)


def _mem_rows_view(cache):
    depth, b = cache.shape[:2]
    c = cache.reshape(depth, b, N_MEM, MEM_HEADS, MEM_SPLIT, LANES)
    return jnp.swapaxes(c, 3, 4).reshape(depth, b, N_MEM * MEM_ROWS, LANES)


PEER_DEPTH = PEER_TOPK + 1
PEER_VROWS = 24


def _sorting_network(n):
    comps = []

    def merge(lo, m, r):
        step = r * 2
        if step < m:
            merge(lo, m, step)
            merge(lo + r, m, step)
            comps.extend((i, i + r) for i in range(lo + r, lo + m - r, step))
        else:
            comps.append((lo, lo + r))

    def sort(lo, m):
        if m > 1:
            sort(lo, m // 2)
            sort(lo + m // 2, m // 2)
            merge(lo, m, 1)

    sort(0, n)
    return comps


def _peer_select_kernel(x_ref, g_ref, wqt_ref, sk_ref, ht_ref, d_ref, e1_ref, s2_ref, e2_ref,
                        qt_scr, v_scr, *, tt):
    h = _rms(x_ref[...], g_ref[...])
    ht = h.T.astype(BF16)
    ht_ref[...] = ht
    qt_scr[...] = _dot(wqt_ref[...], ht)
    half = PEER_TOPK // 2


    def pop_lists(lists, mask, depth):
        return [jnp.where(mask, lists[t + 1] if t + 1 < len(lists) else NEG_INF, lists[t])
                for t in range(depth)]

    def sorted_lists(s):
        v = [s[g * SUBLANES:(g + 1) * SUBLANES, :] for g in range(PEER_NKEYS // SUBLANES)]
        for i, j in _sorting_network(len(v)):
            v[i], v[j] = jnp.maximum(v[i], v[j]), jnp.minimum(v[i], v[j])
        return v

    def value_step(k, lists2, ls, vb):
        out = []
        for slot, v in enumerate(lists2):
            m = jnp.max(v[0], axis=0, keepdims=True)
            v_scr[vb + slot, k:k + 1, ls] = m
            if k + 1 < PEER_DEPTH:
                v = pop_lists(v, v[0] == m, PEER_TOPK - k)
            out.append(v)
        return out

    def candidates(ls, vb):
        v1 = v_scr[vb, :, ls]
        v2 = v_scr[vb + 1, :, ls]
        top = v1[0:half, :]
        first = jnp.where(lax.broadcasted_iota(jnp.int32, top.shape, 0) == 0, top, NEG_INF)
        lists = [(top if b < half else first) + v2[b:b + 1, :] for b in range(PEER_DEPTH)]
        singles = [v1[half:PEER_TOPK, :] + v2[0:1, :], v1[PEER_TOPK:PEER_VROWS, :] + v2[0:1, :]]
        return (lists, singles), v1[0:1, :], v2[0:1, :]

    def cand_step(k, state, tops):
        lists, singles = state
        m = jnp.max(functools.reduce(jnp.maximum, [lists[0]] + singles), axis=0, keepdims=True)
        tops.append(m)
        if k + 1 < PEER_DEPTH:
            lists = pop_lists(lists, lists[0] == m, PEER_TOPK - k)
            singles = [jnp.where(sg == m, NEG_INF, sg) for sg in singles]
        return lists, singles

    def finish(hh, tops, max1, max2, s1, s2, ls):
        z = functools.reduce(lambda a, b: a + b, [jnp.exp(t - tops[0]) for t in tops[:PEER_TOPK]])
        thr = 0.5 * (tops[PEER_TOPK - 1] + tops[PEER_TOPK])
        row = pl.multiple_of(hh * PEER_NKEYS, PEER_NKEYS)
        d_ref[pl.ds(row, PEER_NKEYS), ls] = thr - s1
        e1_ref[pl.ds(row, PEER_NKEYS), ls] = jnp.exp(s1 - max1) / z
        s2_ref[pl.ds(row, PEER_NKEYS), ls] = s2
        e2_ref[pl.ds(row, PEER_NKEYS), ls] = jnp.exp(s2 - max2)

    def head_pair(it, carry):
        head(2 * it, 0)
        head(2 * it + 1, 2)
        return carry

    def head(hh, vb):
        base = pl.multiple_of(hh * 2 * PEER_NKEYS, 2 * PEER_NKEYS)
        q1 = qt_scr[pl.ds(base, PEER_NKEYS), :].astype(BF16)
        q2 = qt_scr[pl.ds(base + PEER_NKEYS, PEER_NKEYS), :].astype(BF16)
        s1 = _dot(sk_ref[2 * hh], q1)
        s2 = _dot(sk_ref[2 * hh + 1], q2)
        lanes = [slice(t0, t0 + LANES) for t0 in range(0, tt, LANES)]
        prev = None
        for ls in lanes + [None]:
            ss = [sorted_lists(s1[:, ls]), sorted_lists(s2[:, ls])] if ls is not None else None
            if prev is not None:
                cands, max1, max2 = candidates(prev, vb)
                tops = []
            for k in range(PEER_DEPTH):
                if ss is not None:
                    ss = value_step(k, ss, ls, vb)
                if prev is not None:
                    cands = cand_step(k, cands, tops)
            if prev is not None:
                finish(hh, tops, max1, max2, s1[:, prev], s2[:, prev], prev)
            prev = ls

    v_scr[:, PEER_TOPK:, :] = jnp.full((4, PEER_VROWS - PEER_TOPK, tt), NEG_INF, F32)
    lax.fori_loop(0, PEER_HEADS // 2, head_pair, 0)


def peer_select(x, gain, wq_t, subkeys, *, tt=512):
    n, d = x.shape
    tt = min(tt, n)
    assert n % tt == 0
    rows = PEER_HEADS * PEER_NKEYS
    col = lambda r: pl.BlockSpec((r, tt), lambda i: (0, i))
    kern = functools.partial(_peer_select_kernel, tt=tt)
    return pl.pallas_call(
        kern, name="peer_select", grid=(n // tt,),
        in_specs=[pl.BlockSpec((tt, d), lambda i: (i, 0)),
                  pl.BlockSpec((1, d), lambda i: (0, 0)),
                  pl.BlockSpec(wq_t.shape, lambda i: (0, 0)),
                  pl.BlockSpec(subkeys.shape, lambda i: (0, 0, 0))],
        out_specs=[col(d), col(rows), col(rows), col(rows), col(rows)],
        out_shape=[jax.ShapeDtypeStruct((d, n), BF16)] + [jax.ShapeDtypeStruct((rows, n), F32)] * 4,
        scratch_shapes=[pltpu.VMEM((2 * rows, tt), F32), pltpu.VMEM((4, PEER_VROWS, tt), F32)],
        compiler_params=_params(("parallel",)))(x, gain.reshape(1, d), wq_t, subkeys)


PEER_CHUNK = 512
PEER_DENSE_FLAGS = None


def _peer_dense_kernel(x_ref, ht_ref, d_ref, e1_ref, s2_ref, e2_ref, u0_ref, un_ref, vtp_ref, vtl_ref, o_ref,
                       acc_scr, a0_scr, a1_scr, p0_scr, p1_scr, *, tt):
    a_scr = (a0_scr, a1_scr)
    p_scr = (p0_scr, p1_scr)
    c = pl.program_id(1)
    last = pl.num_programs(1) - 1
    rows_per_chunk = PEER_CHUNK // PEER_NKEYS
    groups = PEER_NKEYS // SUBLANES

    @pl.when(c == 0)
    def _():
        acc_scr[...] = jnp.zeros(acc_scr.shape, F32)
        p_scr[1][...] = jnp.zeros(p_scr[1].shape, BF16)
        a_scr[0][...] = _dot(u0_ref[...], ht_ref[...])

    def key_rows(r, ls):
        i = c * rows_per_chunk + r
        d_rows = [jnp.broadcast_to(d_ref[pl.ds(hh * PEER_NKEYS + i, 1), ls], (SUBLANES, MXU_WIDTH))
                  for hh in range(PEER_HEADS)]
        e_rows = [jnp.broadcast_to(e1_ref[pl.ds(hh * PEER_NKEYS + i, 1), ls], (SUBLANES, MXU_WIDTH))
                  for hh in range(PEER_HEADS)]
        return d_rows, e_rows

    def weights_item(r, g2, ls, cur, rows):
        d_rows, e_rows = rows
        vals = []
        for g in (g2, g2 + 1):
            w = None
            for hh in range(PEER_HEADS):
                js = slice(hh * PEER_NKEYS + g * SUBLANES, hh * PEER_NKEYS + (g + 1) * SUBLANES)
                sel = jnp.where(s2_ref[js, ls] >= d_rows[hh], e_rows[hh] * e2_ref[js, ls], 0.0)
                w = sel if w is None else w + sel
            a = a_scr[cur][r * PEER_NKEYS + g * SUBLANES:r * PEER_NKEYS + (g + 1) * SUBLANES, ls]
            vals.append(_gelu(a) * w)
        lo = r * PEER_NKEYS + g2 * SUBLANES
        p_scr[cur][lo:lo + 2 * SUBLANES, ls] = jnp.concatenate(vals, axis=0).astype(BF16)

    def preact_piece(k, ls, nxt):
        ks = slice(k * MXU_WIDTH, (k + 1) * MXU_WIDTH)
        part = _dot(un_ref[:, ks], ht_ref[ks, ls])
        if k == 0:
            a_scr[nxt][:, ls] = part
        else:
            a_scr[nxt][:, ls] += part

    def value_piece(k, ls, nxt):
        ks = slice(k * MXU_WIDTH, (k + 1) * MXU_WIDTH)
        acc_scr[:, ls] += _dot(vtp_ref[:, ks], p_scr[nxt][ks, ls])

    def step(cur, nxt):
        n_pre = ht_ref.shape[0] // MXU_WIDTH
        n_val = PEER_CHUNK // MXU_WIDTH
        merged = sorted([((k + 0.5) / n_pre, 0, preact_piece, k) for k in range(n_pre)]
                        + [((k + 0.5) / n_val, 1, value_piece, k) for k in range(n_val)],
                        key=lambda e: e[:2])
        pieces = {r: [] for r in range(rows_per_chunk)}
        for idx, (_, _, fn, k) in enumerate(merged):
            pieces[idx * rows_per_chunk // len(merged)].append((fn, k))
        for t0 in range(0, tt, MXU_WIDTH):
            ls = slice(t0, t0 + MXU_WIDTH)
            for r in range(rows_per_chunk):
                for fn, k in pieces[r]:
                    fn(k, ls, nxt)
                rows = key_rows(r, ls)
                for g2 in range(0, groups, 2):
                    weights_item(r, g2, ls, cur, rows)

    @pl.when((c & 1) == 0)
    def _():
        step(0, 1)

    @pl.when((c & 1) == 1)
    def _():
        step(1, 0)

    @pl.when(c == last)
    def _():
        acc = acc_scr[...] + _dot(vtl_ref[...], p_scr[1][...])
        o_ref[...] = x_ref[...] + acc.T


def peer_dense(x, ht, d_thr, e1, s2, e2, u_tab, v_tab_t, *, tt=512):
    n, d = x.shape
    tt = min(tt, n)
    assert n % tt == 0 and tt % MXU_WIDTH == 0
    rows = PEER_HEADS * PEER_NKEYS
    nchunks = PEER_EXPERTS // PEER_CHUNK
    assert nchunks % 2 == 0
    col = lambda r: pl.BlockSpec((r, tt), lambda i, c: (0, i))
    return pl.pallas_call(
        functools.partial(_peer_dense_kernel, tt=tt), name="peer_dense",
        grid=(n // tt, nchunks),
        in_specs=[pl.BlockSpec((tt, d), lambda i, c: (i, 0)),
                  col(d), col(rows), col(rows), col(rows), col(rows),
                  pl.BlockSpec((PEER_CHUNK, d), lambda i, c: (0, 0)),
                  pl.BlockSpec((PEER_CHUNK, d), lambda i, c: (jnp.minimum(c + 1, nchunks - 1), 0)),
                  pl.BlockSpec((d, PEER_CHUNK), lambda i, c: (0, jnp.maximum(c - 1, 0))),
                  pl.BlockSpec((d, PEER_CHUNK), lambda i, c: (0, nchunks - 1))],
        out_specs=pl.BlockSpec((tt, d), lambda i, c: (i, 0)),
        out_shape=jax.ShapeDtypeStruct((n, d), F32),
        scratch_shapes=[pltpu.VMEM((d, tt), F32)] + [pltpu.VMEM((PEER_CHUNK, tt), F32)] * 2
        + [pltpu.VMEM((PEER_CHUNK, tt), BF16)] * 2,
        compiler_params=_params(("parallel", "arbitrary"), flags=PEER_DENSE_FLAGS))(
            x, ht, d_thr, e1, s2, e2, u_tab, u_tab, v_tab_t, v_tab_t)


def _norm_kernel(x_ref, g_ref, o_ref):
    o_ref[...] = _rms(x_ref[...], g_ref[...])


def final_norm(x, gain, *, tn=512):
    n, d = x.shape
    tn = min(tn, n)
    assert n % tn == 0
    tile = pl.BlockSpec((tn, d), lambda i: (i, 0))
    return pl.pallas_call(
        _norm_kernel, name="final_norm", grid=(n // tn,),
        in_specs=[tile, pl.BlockSpec((1, d), lambda i: (0, 0))], out_specs=tile,
        out_shape=jax.ShapeDtypeStruct((n, d), F32),
        compiler_params=_params(("parallel",)))(x, gain.reshape(1, d))


def _trunk(x, pos0, mem_k, mem_v, pool_past, conv_past, cache_k, cache_v, page_table, W):
    b, t, d = x.shape
    n = b * t
    x = x.reshape(n, d)
    new_k, new_v, new_pool, new_conv = [], [], [], []
    for l in range(DEPTH):
        i = l // 2
        if l % 2 == 0:
            lam_init = 0.8 - 0.6 * math.exp(-0.3 * l)
            q, k, v, p = fused_linear([x], [W["w_in_ab"][i]], name="mix_in_ab", gain=W["norm_mix"][l],
                                      splits=[A_QK] * 4)
            q3, k3, v3, p3 = (a.reshape(b, t, A_QK) for a in (q, k, v, p))
            lam_vecs = [W[nm][i] for nm in ("lambda_q1", "lambda_k1", "lambda_q2", "lambda_k2")]
            if page_table is None:
                o = dattn_prompt(q3, k3, v3, lam_vecs, W["subln_g"][i], lam_init)
            else:
                o = dattn_sample(q3, k3, v3, cache_k, cache_v, i, page_table, lam_vecs,
                                 W["subln_g"][i], lam_init)
            past = pool_past[i]
            past16 = jnp.pad(past, ((0, 0), (1, 0), (0, 0)))
            pooled = pool_mix(p3, past16, pos0, W["pool_w"][i], W["pool_scale"][i])
            x = fused_linear([o.reshape(n, A_QK), pooled.reshape(n, POOL_DIM)],
                             [W["w_out_ab"][i][:A_QK], W["w_out_ab"][i][A_QK:]], name="mix_out_ab", res=x)
            new_k.append(k3.reshape(b, t, A_HEADS, A_HD))
            new_v.append(v3.reshape(b, t, A_HEADS, A_HD))
            new_pool.append(jnp.concatenate([past, p3], axis=1)[:, -(POOL_MAX - 1):])
        else:
            bg, u = fused_linear([x], [W["w_in_c"][i]], name="mix_in_c", gain=W["norm_mix"][l], gate=True)
            u3 = u.reshape(b, t, d)
            past = conv_past[i]
            past8 = jnp.pad(past, ((0, 0), (CONV_HALO - (CONV_W - 1), 0), (0, 0)))
            x = short_conv(u3, bg.reshape(b, t, d), past8, W["conv_w"][i], W["w_out_c"][i],
                           x.reshape(b, t, d)).reshape(n, d)
            new_conv.append(jnp.concatenate([past, u3], axis=1)[:, -(CONV_W - 1):])
        if page_table is None:
            x = mem_block(x.reshape(b, t, d), W["norm_mem"][l], W["w_mq"][l], mem_k, mem_v, W["w_mo"][l],
                          l).reshape(n, d)
        else:
            qm = fused_linear([x], [W["w_mq"][l]], name="mem_q", gain=W["norm_mem"][l])
            om = mem_attn_rows(qm.reshape(b, t, d), mem_k, mem_v, l)
            x = fused_linear([om.reshape(n, d)], [W["w_mo"][l]], name="mem_o", res=x)
        sel = peer_select(x, W["norm_ffn"][l], W["peer_wq_t"][l], W["peer_subkeys"][l])
        x = peer_dense(x, *sel, W["peer_u"][l], W["peer_v_t"][l])
    y = final_norm(x, W["norm_final"]).reshape(b, t, d)
    return y, jnp.stack(new_k), jnp.stack(new_v), jnp.stack(new_pool), jnp.stack(new_conv)


def _bf(w):
    return w.astype(BF16)


def _prepare_weights(**w):
    out = dict(w)
    for name in ("w_in_ab", "w_out_ab", "pool_w", "w_in_c", "w_out_c", "w_mq", "w_mo", "peer_u"):
        out[name] = _bf(w[name])
    out["peer_wq_t"] = _bf(jnp.swapaxes(w["peer_wq"], 1, 2))
    out["peer_subkeys"] = _bf(w["peer_subkeys"].reshape(DEPTH, PEER_HEADS * 2, PEER_NKEYS, PEER_NKEYS))
    out["peer_v_t"] = _bf(jnp.swapaxes(w["peer_v"], 1, 2))
    return out


def kernel(x_prompt, x_sample, cache_attn_k, cache_attn_v, state_pool, state_conv, cache_mem_k, cache_mem_v, page_table, mem_prompt, norm_mix, norm_mem, norm_ffn, norm_final, w_in_ab, w_out_ab, lambda_q1, lambda_k1, lambda_q2, lambda_k2, subln_g, pool_w, pool_scale, w_in_c, conv_w, w_out_c, w_mq, w_mk, w_mv, w_mo, peer_wq, peer_subkeys, peer_u, peer_v):
    W = _prepare_weights(
        norm_mix=norm_mix, norm_mem=norm_mem, norm_ffn=norm_ffn, norm_final=norm_final, w_in_ab=w_in_ab,
        w_out_ab=w_out_ab, lambda_q1=lambda_q1, lambda_k1=lambda_k1, lambda_q2=lambda_q2, lambda_k2=lambda_k2,
        subln_g=subln_g, pool_w=pool_w, pool_scale=pool_scale, w_in_c=w_in_c, conv_w=conv_w, w_out_c=w_out_c,
        w_mq=w_mq, w_mo=w_mo, peer_wq=peer_wq, peer_subkeys=peer_subkeys, peer_u=peer_u, peer_v=peer_v)
    bp, sp, d = x_prompt.shape
    mk_p, mv_p = mem_project(mem_prompt.reshape(bp * N_MEM, d), _bf(w_mk), _bf(w_mv))
    mk_p = mk_p.reshape(DEPTH, bp, N_MEM, d)
    mv_p = mv_p.reshape(DEPTH, bp, N_MEM, d)
    pool0 = jnp.zeros((w_in_ab.shape[0], bp, POOL_MAX - 1, POOL_DIM), F32)
    conv0 = jnp.zeros((w_in_c.shape[0], bp, CONV_W - 1, d), F32)
    y_p, k_p, v_p, pool_p, conv_p = _trunk(x_prompt, 0, mk_p, mv_p, pool0, conv0, None, None, None, W)
    past_len = page_table.shape[1] * cache_attn_k.shape[2]
    n_layers, n_pool = cache_attn_k.shape[:2]
    ck = cache_attn_k.reshape(n_layers, n_pool, PAGE_SIZE * A_HEADS, A_HD)
    cv = cache_attn_v.reshape(n_layers, n_pool, PAGE_SIZE * A_HEADS, A_HD)
    y_s, k_s, v_s, pool_s, conv_s = _trunk(
        x_sample, past_len, _mem_rows_view(cache_mem_k), _mem_rows_view(cache_mem_v), state_pool, state_conv,
        ck, cv, page_table, W)
    mshape = (DEPTH, bp, N_MEM, MEM_HEADS, MEM_DH)
    return (y_p, y_s, k_p, v_p, pool_p, conv_p, mk_p.reshape(mshape), mv_p.reshape(mshape),
            k_s, v_s, pool_s, conv_s)
```

```python
import functools
import math

import jax
import jax.numpy as jnp
from jax import lax
from jax.experimental import pallas as pl
from jax.experimental.pallas import tpu as pltpu

F32 = jnp.float32
BF16 = jnp.bfloat16

D_MODEL = 1024
DEPTH = 4
EPS = 1e-5
A_HEADS = 4
A_DH = 64
A_HD = 2 * A_DH
A_QK = A_HEADS * A_HD
POOL_WINDOWS = (2, 4, 8, 16)
POOL_GDIM = 128
POOL_DIM = 512
POOL_MAX = 16
CONV_W = 3
N_MEM = 256
MEM_HEADS = 4
MEM_DH = 256
PEER_HEADS = 8
PEER_NKEYS = 128
PEER_EXPERTS = PEER_NKEYS * PEER_NKEYS
PEER_TOPK = 16
PAGE_SIZE = 128

SUBLANES = 8
LANES = 128
MXU_WIDTH = 256
VMEM_LIMIT = 56 * 1024 * 1024
NEG_INF = float("-inf")
MASK_VALUE = float(jnp.finfo(jnp.float32).min)
LOG2E = 1.4426950408889634


def _params(sem, vmem=VMEM_LIMIT, flags=None):
    return pltpu.CompilerParams(dimension_semantics=sem, vmem_limit_bytes=vmem, flags=flags)


def _rms(x, g):
    return x * lax.rsqrt(jnp.mean(x * x, axis=-1, keepdims=True) + EPS) * g


def _gelu(a):
    return 0.5 * a * (1.0 + lax.erf(a * (2.0 ** -0.5)))


def _dot(a, b):
    return jnp.dot(a, b, preferred_element_type=F32)


def _dot_nt(a, b):
    return lax.dot_general(a, b, (((1,), (1,)), ((), ())), preferred_element_type=F32)


def _linear_kernel(*refs, n_in, has_norm, has_res, splits, gate):
    a_refs = refs[:n_in]
    w_refs = refs[n_in:2 * n_in]
    idx = 2 * n_in
    g_ref = refs[idx] if has_norm else None
    idx += int(has_norm)
    r_ref = refs[idx] if has_res else None
    idx += int(has_res)
    out_refs = refs[idx:]
    acc = None
    for a_ref, w_ref in zip(a_refs, w_refs):
        a = a_ref[...]
        if has_norm:
            a = _rms(a, g_ref[...])
        d = _dot(a.astype(BF16), w_ref[...])
        acc = d if acc is None else acc + d
    if has_res:
        acc = acc + r_ref[...]
    if gate:
        d3 = acc.shape[1] // 3
        out_refs[0][...] = acc[:, :d3]
        out_refs[1][...] = acc[:, d3:2 * d3] * acc[:, 2 * d3:]
    else:
        off = 0
        for o_ref, s in zip(out_refs, splits):
            o_ref[...] = acc[:, off:off + s]
            off += s


def fused_linear(a_list, w_list, *, name, gain=None, res=None, splits=None, gate=False, tn=512):
    n = a_list[0].shape[0]
    m = w_list[0].shape[1]
    tn = min(tn, n)
    assert n % tn == 0
    if gate:
        out_cols = [m // 3, m // 3]
    else:
        out_cols = list(splits) if splits is not None else [m]
        assert sum(out_cols) == m
    in_specs, args = [], []
    for a in a_list:
        in_specs.append(pl.BlockSpec((tn, a.shape[1]), lambda i: (i, 0)))
        args.append(a)
    for w in w_list:
        in_specs.append(pl.BlockSpec(w.shape, lambda i: (0, 0)))
        args.append(w)
    if gain is not None:
        in_specs.append(pl.BlockSpec((1, gain.shape[-1]), lambda i: (0, 0)))
        args.append(gain.reshape(1, -1))
    if res is not None:
        in_specs.append(pl.BlockSpec((tn, m), lambda i: (i, 0)))
        args.append(res)
    out_shape = [jax.ShapeDtypeStruct((n, c), F32) for c in out_cols]
    out_specs = [pl.BlockSpec((tn, c), lambda i: (i, 0)) for c in out_cols]
    kern = functools.partial(_linear_kernel, n_in=len(a_list), has_norm=gain is not None,
                             has_res=res is not None, splits=tuple(out_cols), gate=gate)
    outs = pl.pallas_call(
        kern, name=name, grid=(n // tn,), in_specs=in_specs, out_specs=out_specs, out_shape=out_shape,
        compiler_params=_params(("parallel",)))(*args)
    return outs if len(outs) > 1 else outs[0]


def _memproj_kernel(a_ref, wk_ref, wv_ref, ok_ref, ov_ref):
    a = a_ref[...].astype(BF16)
    ok_ref[0] = _dot(a, wk_ref[0])
    ov_ref[0] = _dot(a, wv_ref[0])


def mem_project(mem2d, wk, wv):
    n, d = mem2d.shape
    depth = wk.shape[0]
    spec_w = pl.BlockSpec((1, d, d), lambda l: (l, 0, 0))
    spec_o = pl.BlockSpec((1, n, d), lambda l: (l, 0, 0))
    return pl.pallas_call(
        _memproj_kernel, name="mem_project", grid=(depth,),
        in_specs=[pl.BlockSpec((n, d), lambda l: (0, 0)), spec_w, spec_w],
        out_specs=[spec_o, spec_o],
        out_shape=[jax.ShapeDtypeStruct((depth, n, d), F32)] * 2,
        compiler_params=_params(("parallel",)))(mem2d, wk, wv)


def _lambda_value(lq1, lk1, lq2, lk2, lam_init):
    return (jnp.exp(jnp.sum(lq1[...] * lk1[...], axis=-1, keepdims=True))
            - jnp.exp(jnp.sum(lq2[...] * lk2[...], axis=-1, keepdims=True)) + lam_init)


def _subln(o, g, lam_init):
    return _rms(o, g) * (1.0 - lam_init)


def _dattn_prompt_kernel(lq1, lk1, lq2, lk2, g_ref, q_ref, k_ref, v_ref, o_ref, kb_scr, vt_scr, *,
                         tq, lam_init):
    qi = pl.program_id(2)
    nblk = kb_scr.shape[0]

    @pl.when(qi == 0)
    def _():
        for jb in range(nblk):
            kb_scr[jb] = k_ref[0, jb * tq:(jb + 1) * tq, :].astype(BF16)
            vt_scr[jb] = v_ref[0, jb * tq:(jb + 1) * tq, :].T.astype(BF16)

    qt = (q_ref[0] * (A_DH ** -0.5 * LOG2E)).T
    sub = lax.broadcasted_iota(jnp.int32, (A_HD, tq), 0)
    qz = (jnp.where(sub < A_DH, qt, 0.0).astype(BF16), jnp.where(sub >= A_DH, qt, 0.0).astype(BF16))

    def blocks(js, carry, masked):
        kbs = [kb_scr[j] for j in js]
        vts = [vt_scr[j] for j in js]
        out = []
        scores = [[_dot(kb, qz[mi]) for kb in kbs] for mi in range(2)]
        for mi in range(2):
            m, l, a = carry[3 * mi:3 * mi + 3]
            ss = scores[mi]
            if masked:
                kk = lax.broadcasted_iota(jnp.int32, (tq, tq), 0)
                qq = lax.broadcasted_iota(jnp.int32, (tq, tq), 1)
                ss = [jnp.where(kk <= qq, s, MASK_VALUE) for s in ss]
            m_new = functools.reduce(jnp.maximum, [m] + [jnp.max(s, axis=0, keepdims=True) for s in ss])
            alpha = jnp.exp2(m - m_new)
            l = alpha * l
            a = alpha * a
            for s, vt in zip(ss, vts):
                p = jnp.exp2(s - m_new)
                l = l + jnp.sum(p, axis=0, keepdims=True)
                a = a + _dot(vt, p.astype(BF16))
            out += [m_new, l, a]
        return tuple(out)

    init = (jnp.full((1, tq), NEG_INF, F32), jnp.zeros((1, tq), F32), jnp.zeros((A_HD, tq), F32)) * 2
    carry = lax.fori_loop(0, qi >> 1, lambda j, c: blocks((2 * j, 2 * j + 1), c, False), init)
    carry = lax.fori_loop(0, qi & 1, lambda j, c: blocks((qi - 1,), c, False), carry)
    m1, l1, a1, m2, l2, a2 = blocks((qi,), carry, True)
    lam = _lambda_value(lq1, lk1, lq2, lk2, lam_init)
    ot = a1 / l1 - lam * (a2 / l2)
    o_ref[0] = _subln(ot.T, g_ref[...], lam_init)


def dattn_prompt(q, k, v, lam_vecs, subln_g, lam_init, *, tq=512):
    b, t, _ = q.shape
    tq = min(tq, t)
    assert t % tq == 0
    vec = pl.BlockSpec((1, A_DH), lambda bi, h, qi: (0, 0))
    kern = functools.partial(_dattn_prompt_kernel, tq=tq, lam_init=lam_init)
    return pl.pallas_call(
        kern, name="dattn_prompt", grid=(b, A_HEADS, t // tq),
        in_specs=[vec, vec, vec, vec,
                  pl.BlockSpec((1, A_HD), lambda bi, h, qi: (0, 0)),
                  pl.BlockSpec((1, tq, A_HD), lambda bi, h, qi: (bi, qi, h)),
                  pl.BlockSpec((1, t, A_HD), lambda bi, h, qi: (bi, 0, h)),
                  pl.BlockSpec((1, t, A_HD), lambda bi, h, qi: (bi, 0, h))],
        out_specs=pl.BlockSpec((1, tq, A_HD), lambda bi, h, qi: (bi, qi, h)),
        out_shape=jax.ShapeDtypeStruct((b, t, A_QK), F32),
        scratch_shapes=[pltpu.VMEM((t // tq, tq, A_HD), BF16), pltpu.VMEM((t // tq, A_HD, tq), BF16)],
        compiler_params=_params(("parallel", "parallel", "arbitrary")))(
            *[x.reshape(1, A_DH) for x in lam_vecs], subln_g.reshape(1, A_HD), q, k, v)


def _dattn_sample_kernel(pt_ref, lq1, lk1, lq2, lk2, g_ref, q_ref, kn_ref, vn_ref, *rest,
                         n_pages, t_new, lam_init):
    k_refs = rest[:n_pages]
    v_refs = rest[n_pages:2 * n_pages]
    o_ref = rest[2 * n_pages]
    rows = A_HEADS * 2 * t_new
    hbits = A_HEADS.bit_length() - 1
    tbits = t_new.bit_length() - 1
    q = q_ref[0] * (A_DH ** -0.5)
    lane = lax.broadcasted_iota(jnp.int32, (t_new, A_HD), 1)
    parts = []
    for h in range(A_HEADS):
        qh = q[:, h * A_HD:(h + 1) * A_HD]
        parts += [jnp.where(lane < A_DH, qh, 0.0), jnp.where(lane >= A_DH, qh, 0.0)]
    qz = jnp.concatenate(parts, axis=0).astype(BF16)

    def bias(ncols, causal):
        rr = lax.broadcasted_iota(jnp.int32, (rows, ncols), 0)
        cc = lax.broadcasted_iota(jnp.int32, (rows, ncols), 1)
        ok = (cc & (A_HEADS - 1)) == (rr >> (tbits + 1))
        if causal:
            ok = ok & ((cc >> hbits) <= (rr & (t_new - 1)))
        return jnp.where(ok, 0.0, NEG_INF)

    page_bias = bias(PAGE_SIZE * A_HEADS, False)
    s_pages = [_dot_nt(qz, k_ref[0, 0].astype(BF16)) + page_bias for k_ref in k_refs]
    s_new = _dot_nt(qz, kn_ref[0].astype(BF16)) + bias(t_new * A_HEADS, True)
    m = jnp.max(functools.reduce(jnp.maximum, s_pages), axis=-1, keepdims=True)
    m = jnp.maximum(m, jnp.max(s_new, axis=-1, keepdims=True))
    p_new = jnp.exp(s_new - m)
    l = jnp.sum(p_new, axis=-1, keepdims=True)
    acc = _dot(p_new.astype(BF16), vn_ref[0].astype(BF16))
    for s, v_ref in zip(s_pages, v_refs):
        p = jnp.exp(s - m)
        l = l + jnp.sum(p, axis=-1, keepdims=True)
        acc = acc + _dot(p.astype(BF16), v_ref[0, 0].astype(BF16))
    acc = acc / l
    lam = _lambda_value(lq1, lk1, lq2, lk2, lam_init)
    for h in range(A_HEADS):
        r0 = h * 2 * t_new
        o = acc[r0:r0 + t_new] - lam * acc[r0 + t_new:r0 + 2 * t_new]
        o_ref[0, :, h * A_HD:(h + 1) * A_HD] = _subln(o, g_ref[...], lam_init)


def dattn_sample(q, k_new, v_new, cache_k, cache_v, layer, page_table, lam_vecs, subln_g, lam_init):
    b, t_new, _ = q.shape
    assert t_new & (t_new - 1) == 0
    n_pages = page_table.shape[1]
    pt = page_table.reshape(-1)
    vec = pl.BlockSpec((1, A_DH), lambda bi, pt: (0, 0))
    qspec = pl.BlockSpec((1, t_new, A_QK), lambda bi, pt: (bi, 0, 0))
    nspec = pl.BlockSpec((1, t_new * A_HEADS, A_HD), lambda bi, pt: (bi, 0, 0))

    def page(r):
        return pl.BlockSpec((1, 1, PAGE_SIZE * A_HEADS, A_HD),
                            lambda bi, pt: (layer, pt[bi * n_pages + r], 0, 0))

    pages = [page(r) for r in range(n_pages)]
    kern = functools.partial(_dattn_sample_kernel, n_pages=n_pages, t_new=t_new, lam_init=lam_init)
    grid_spec = pltpu.PrefetchScalarGridSpec(
        num_scalar_prefetch=1, grid=(b,),
        in_specs=[vec, vec, vec, vec, pl.BlockSpec((1, A_HD), lambda bi, pt: (0, 0)),
                  qspec, nspec, nspec] + pages + pages,
        out_specs=qspec)
    return pl.pallas_call(
        kern, name="dattn_sample", grid_spec=grid_spec,
        out_shape=jax.ShapeDtypeStruct((b, t_new, A_QK), F32),
        compiler_params=_params(("parallel",)))(
            pt, *[x.reshape(1, A_DH) for x in lam_vecs], subln_g.reshape(1, A_HD), q,
            k_new.reshape(b, t_new * A_HEADS, A_HD), v_new.reshape(b, t_new * A_HEADS, A_HD),
            *([cache_k] * n_pages), *([cache_v] * n_pages))


def _pool_kernel(p_ref, halo_ref, past_ref, w_ref, sc_ref, o_ref, scr, *, tt, pos0):
    ti = pl.program_id(1)
    scr[0:POOL_MAX, :] = jnp.where(ti == 0, past_ref[0], halo_ref[0])
    scr[POOL_MAX:, :] = p_ref[0]
    pos = pos0 + ti * tt + lax.broadcasted_iota(jnp.int32, (tt, POOL_GDIM), 0)
    for g, w in enumerate(POOL_WINDOWS):
        sl = slice(g * POOL_GDIM, (g + 1) * POOL_GDIM)
        x = scr[POOL_MAX:, sl]
        s = x
        for jj in range(1, w):
            s = s + scr[POOL_MAX - jj:POOL_MAX - jj + tt, sl]
        cnt = jnp.minimum(w, pos + 1).astype(F32)
        d = s / cnt - x
        y = _dot(d.astype(BF16), w_ref[g])
        o_ref[0, :, sl] = y * sc_ref[:, sl]


def pool_mix(p, past16, pos0, pool_w, pool_scale, *, tt=1024):
    b, t, _ = p.shape
    tt = min(tt, t)
    halo_src = p if t >= POOL_MAX else past16
    hb = tt // POOL_MAX
    kern = functools.partial(_pool_kernel, tt=tt, pos0=pos0)
    return pl.pallas_call(
        kern, name="pool_mix", grid=(b, t // tt),
        in_specs=[pl.BlockSpec((1, tt, POOL_DIM), lambda bi, ti: (bi, ti, 0)),
                  pl.BlockSpec((1, POOL_MAX, POOL_DIM), lambda bi, ti: (bi, jnp.maximum(ti * hb - 1, 0), 0)),
                  pl.BlockSpec((1, POOL_MAX, POOL_DIM), lambda bi, ti: (bi, 0, 0)),
                  pl.BlockSpec(pool_w.shape, lambda bi, ti: (0, 0, 0)),
                  pl.BlockSpec((1, POOL_DIM), lambda bi, ti: (0, 0))],
        out_specs=pl.BlockSpec((1, tt, POOL_DIM), lambda bi, ti: (bi, ti, 0)),
        out_shape=jax.ShapeDtypeStruct((b, t, POOL_DIM), F32),
        scratch_shapes=[pltpu.VMEM((POOL_MAX + tt, POOL_DIM), F32)],
        compiler_params=_params(("parallel", "arbitrary")))(
            p, halo_src, past16, pool_w, pool_scale.reshape(1, POOL_DIM))


CONV_HALO = 8


def _conv_kernel(u_ref, halo_ref, past_ref, b_ref, cw_ref, w_ref, x_ref, o_ref, scr, *, tt):
    ti = pl.program_id(1)
    scr[0:CONV_HALO, :] = jnp.where(ti == 0, past_ref[0], halo_ref[0])
    scr[CONV_HALO:, :] = u_ref[0]
    y = None
    for jj in range(CONV_W):
        off = CONV_HALO - (CONV_W - 1) + jj
        term = cw_ref[jj:jj + 1, :] * scr[off:off + tt, :]
        y = term if y is None else y + term
    o_ref[0] = x_ref[0] + _dot((b_ref[0] * y).astype(BF16), w_ref[...])


def short_conv(u, bgate, past8, conv_w, w_out, x, *, tt=512):
    b, t, d = u.shape
    tt = min(tt, t)
    hb = tt // CONV_HALO
    tile = pl.BlockSpec((1, tt, d), lambda bi, ti: (bi, ti, 0))
    kern = functools.partial(_conv_kernel, tt=tt)
    return pl.pallas_call(
        kern, name="short_conv", grid=(b, t // tt),
        in_specs=[tile,
                  pl.BlockSpec((1, CONV_HALO, d), lambda bi, ti: (bi, jnp.maximum(ti * hb - 1, 0), 0)),
                  pl.BlockSpec((1, CONV_HALO, d), lambda bi, ti: (bi, 0, 0)),
                  tile,
                  pl.BlockSpec((CONV_W, d), lambda bi, ti: (0, 0)),
                  pl.BlockSpec(w_out.shape, lambda bi, ti: (0, 0)),
                  tile],
        out_specs=tile,
        out_shape=jax.ShapeDtypeStruct((b, t, d), F32),
        scratch_shapes=[pltpu.VMEM((CONV_HALO + tt, d), F32)],
        compiler_params=_params(("parallel", "arbitrary")))(u, u, past8, bgate, conv_w, w_out, x)


def _memblock_kernel(x_ref, g_ref, wq_ref, k_ref, v_ref, wo_ref, o_ref):
    x = x_ref[0]
    q = _dot(_rms(x, g_ref[...]).astype(BF16), wq_ref[...]) * (MEM_DH ** -0.5)
    outs = []
    for h in range(MEM_HEADS):
        sl = slice(h * MEM_DH, (h + 1) * MEM_DH)
        s = _dot_nt(q[:, sl].astype(BF16), k_ref[0, 0, :, sl].astype(BF16))
        s = s - jnp.max(s, axis=-1, keepdims=True)
        p = jnp.exp(s)
        p = p / jnp.sum(p, axis=-1, keepdims=True)
        outs.append(_dot(p.astype(BF16), v_ref[0, 0, :, sl].astype(BF16)).astype(BF16))
    o_ref[0] = x + _dot(jnp.concatenate(outs, axis=1), wo_ref[...])


def mem_block(x, gain, wq, mk, mv, wo, layer, *, tq=512):
    b, t, d = x.shape
    tq = min(tq, t)
    tile = pl.BlockSpec((1, tq, d), lambda bi, ti: (bi, ti, 0))
    mem = pl.BlockSpec((1, 1, N_MEM, d), lambda bi, ti: (layer, bi, 0, 0))
    wspec = pl.BlockSpec((d, d), lambda bi, ti: (0, 0))
    return pl.pallas_call(
        _memblock_kernel, name="mem_block", grid=(b, t // tq),
        in_specs=[tile, pl.BlockSpec((1, d), lambda bi, ti: (0, 0)), wspec, mem, mem, wspec],
        out_specs=tile, out_shape=jax.ShapeDtypeStruct((b, t, d), F32),
        compiler_params=_params(("parallel", "parallel")))(x, gain.reshape(1, d), wq, mk, mv, wo)


MEM_SPLIT = MEM_DH // LANES
MEM_ROWS = MEM_HEADS * MEM_SPLIT


def _memattn_rows_kernel(q_ref, k_ref, v_ref, o_ref, *, t):
    ncol = N_MEM * MEM_ROWS
    nseq = q_ref.shape[0]
    rr = lax.broadcasted_iota(jnp.int32, (MEM_HEADS * t, ncol), 0)
    cc = lax.broadcasted_iota(jnp.int32, (MEM_HEADS * t, ncol), 1)
    valid = (cc & (MEM_ROWS - 1)) == (rr >> (t.bit_length() - 1))

    def scores(g):
        q = q_ref[g] * (MEM_DH ** -0.5)
        qz = jnp.concatenate([q[:, kk * LANES:(kk + 1) * LANES] for kk in range(MEM_ROWS)], axis=0)
        return _dot_nt(qz.astype(BF16), k_ref[0, g].astype(BF16))

    def probs(sfull):
        sfull = sfull.reshape(MEM_HEADS, MEM_SPLIT, t, ncol)
        s = sfull[:, 0]
        for j in range(1, MEM_SPLIT):
            part = sfull[:, j].reshape(MEM_HEADS * t, ncol)
            s = s + pltpu.roll(part, ncol - j * MEM_HEADS, axis=1).reshape(MEM_HEADS, t, ncol)
        s = jnp.where(valid, s.reshape(MEM_HEADS * t, ncol), NEG_INF)
        s = s - jnp.max(s, axis=-1, keepdims=True)
        p = jnp.exp(s)
        p = p / jnp.sum(p, axis=-1, keepdims=True)
        parts = [p.reshape(MEM_HEADS, t, ncol)]
        for j in range(1, MEM_SPLIT):
            parts.append(pltpu.roll(p, j * MEM_HEADS, axis=1).reshape(MEM_HEADS, t, ncol))
        return jnp.stack(parts, axis=1).reshape(MEM_ROWS * t, ncol).astype(BF16)

    all_scores = [scores(g) for g in range(nseq)]
    all_probs = [probs(s) for s in all_scores]
    for g in range(nseq):
        o = _dot(all_probs[g], v_ref[0, g].astype(BF16))
        for kk in range(MEM_ROWS):
            o_ref[g, :, kk * LANES:(kk + 1) * LANES] = o[kk * t:(kk + 1) * t]


def mem_attn_rows(q, mk_rows, mv_rows, layer, *, nseq=4):
    b, t, d = q.shape
    assert t & (t - 1) == 0 and t % SUBLANES == 0 and b % nseq == 0
    tile = pl.BlockSpec((nseq, t, d), lambda bi: (bi, 0, 0))
    mem = pl.BlockSpec((1, nseq, N_MEM * MEM_ROWS, LANES), lambda bi: (layer, bi, 0, 0))
    return pl.pallas_call(
        functools.partial(_memattn_rows_kernel, t=t), name="mem_attn_rows", grid=(b // nseq,),
        in_specs=[tile, mem, mem], out_specs=tile,
        out_shape=jax.ShapeDtypeStruct((b, t, d), F32),
        compiler_params=_params(("parallel",)))(q, mk_rows, mv_rows)


def _mem_rows_view(cache):
    depth, b = cache.shape[:2]
    c = cache.reshape(depth, b, N_MEM, MEM_HEADS, MEM_SPLIT, LANES)
    return jnp.swapaxes(c, 3, 4).reshape(depth, b, N_MEM * MEM_ROWS, LANES)


PEER_DEPTH = PEER_TOPK + 1
PEER_VROWS = 24


def _sorting_network(n):
    comps = []

    def merge(lo, m, r):
        step = r * 2
        if step < m:
            merge(lo, m, step)
            merge(lo + r, m, step)
            comps.extend((i, i + r) for i in range(lo + r, lo + m - r, step))
        else:
            comps.append((lo, lo + r))

    def sort(lo, m):
        if m > 1:
            sort(lo, m // 2)
            sort(lo + m // 2, m // 2)
            merge(lo, m, 1)

    sort(0, n)
    return comps


def _peer_select_kernel(x_ref, g_ref, wqt_ref, sk_ref, ht_ref, d_ref, e1_ref, s2_ref, e2_ref,
                        qt_scr, v_scr, *, tt):
    h = _rms(x_ref[...], g_ref[...])
    ht = h.T.astype(BF16)
    ht_ref[...] = ht
    qt_scr[...] = _dot(wqt_ref[...], ht)
    half = PEER_TOPK // 2


    def pop_lists(lists, mask, depth):
        return [jnp.where(mask, lists[t + 1] if t + 1 < len(lists) else NEG_INF, lists[t])
                for t in range(depth)]

    def sorted_lists(s):
        v = [s[g * SUBLANES:(g + 1) * SUBLANES, :] for g in range(PEER_NKEYS // SUBLANES)]
        for i, j in _sorting_network(len(v)):
            v[i], v[j] = jnp.maximum(v[i], v[j]), jnp.minimum(v[i], v[j])
        return v

    def value_step(k, lists2, ls, vb):
        out = []
        for slot, v in enumerate(lists2):
            m = jnp.max(v[0], axis=0, keepdims=True)
            v_scr[vb + slot, k:k + 1, ls] = m
            if k + 1 < PEER_DEPTH:
                v = pop_lists(v, v[0] == m, PEER_TOPK - k)
            out.append(v)
        return out

    def candidates(ls, vb):
        v1 = v_scr[vb, :, ls]
        v2 = v_scr[vb + 1, :, ls]
        top = v1[0:half, :]
        first = jnp.where(lax.broadcasted_iota(jnp.int32, top.shape, 0) == 0, top, NEG_INF)
        lists = [(top if b < half else first) + v2[b:b + 1, :] for b in range(PEER_DEPTH)]
        singles = [v1[half:PEER_TOPK, :] + v2[0:1, :], v1[PEER_TOPK:PEER_VROWS, :] + v2[0:1, :]]
        return (lists, singles), v1[0:1, :], v2[0:1, :]

    def cand_step(k, state, tops):
        lists, singles = state
        m = jnp.max(functools.reduce(jnp.maximum, [lists[0]] + singles), axis=0, keepdims=True)
        tops.append(m)
        if k + 1 < PEER_DEPTH:
            lists = pop_lists(lists, lists[0] == m, PEER_TOPK - k)
            singles = [jnp.where(sg == m, NEG_INF, sg) for sg in singles]
        return lists, singles

    def finish(hh, tops, max1, max2, s1, s2, ls):
        z = functools.reduce(lambda a, b: a + b, [jnp.exp(t - tops[0]) for t in tops[:PEER_TOPK]])
        thr = 0.5 * (tops[PEER_TOPK - 1] + tops[PEER_TOPK])
        row = pl.multiple_of(hh * PEER_NKEYS, PEER_NKEYS)
        d_ref[pl.ds(row, PEER_NKEYS), ls] = thr - s1
        e1_ref[pl.ds(row, PEER_NKEYS), ls] = jnp.exp(s1 - max1) / z
        s2_ref[pl.ds(row, PEER_NKEYS), ls] = s2
        e2_ref[pl.ds(row, PEER_NKEYS), ls] = jnp.exp(s2 - max2)

    def head_pair(it, carry):
        head(2 * it, 0)
        head(2 * it + 1, 2)
        return carry

    def head(hh, vb):
        base = pl.multiple_of(hh * 2 * PEER_NKEYS, 2 * PEER_NKEYS)
        q1 = qt_scr[pl.ds(base, PEER_NKEYS), :].astype(BF16)
        q2 = qt_scr[pl.ds(base + PEER_NKEYS, PEER_NKEYS), :].astype(BF16)
        s1 = _dot(sk_ref[2 * hh], q1)
        s2 = _dot(sk_ref[2 * hh + 1], q2)
        lanes = [slice(t0, t0 + LANES) for t0 in range(0, tt, LANES)]
        prev = None
        for ls in lanes + [None]:
            ss = [sorted_lists(s1[:, ls]), sorted_lists(s2[:, ls])] if ls is not None else None
            if prev is not None:
                cands, max1, max2 = candidates(prev, vb)
                tops = []
            for k in range(PEER_DEPTH):
                if ss is not None:
                    ss = value_step(k, ss, ls, vb)
                if prev is not None:
                    cands = cand_step(k, cands, tops)
            if prev is not None:
                finish(hh, tops, max1, max2, s1[:, prev], s2[:, prev], prev)
            prev = ls

    v_scr[:, PEER_TOPK:, :] = jnp.full((4, PEER_VROWS - PEER_TOPK, tt), NEG_INF, F32)
    lax.fori_loop(0, PEER_HEADS // 2, head_pair, 0)


def peer_select(x, gain, wq_t, subkeys, *, tt=512):
    n, d = x.shape
    tt = min(tt, n)
    assert n % tt == 0
    rows = PEER_HEADS * PEER_NKEYS
    col = lambda r: pl.BlockSpec((r, tt), lambda i: (0, i))
    kern = functools.partial(_peer_select_kernel, tt=tt)
    return pl.pallas_call(
        kern, name="peer_select", grid=(n // tt,),
        in_specs=[pl.BlockSpec((tt, d), lambda i: (i, 0)),
                  pl.BlockSpec((1, d), lambda i: (0, 0)),
                  pl.BlockSpec(wq_t.shape, lambda i: (0, 0)),
                  pl.BlockSpec(subkeys.shape, lambda i: (0, 0, 0))],
        out_specs=[col(d), col(rows), col(rows), col(rows), col(rows)],
        out_shape=[jax.ShapeDtypeStruct((d, n), BF16)] + [jax.ShapeDtypeStruct((rows, n), F32)] * 4,
        scratch_shapes=[pltpu.VMEM((2 * rows, tt), F32), pltpu.VMEM((4, PEER_VROWS, tt), F32)],
        compiler_params=_params(("parallel",)))(x, gain.reshape(1, d), wq_t, subkeys)


PEER_CHUNK = 512
PEER_DENSE_FLAGS = None


def _peer_dense_kernel(x_ref, ht_ref, d_ref, e1_ref, s2_ref, e2_ref, u0_ref, un_ref, vtp_ref, vtl_ref, o_ref,
                       acc_scr, a0_scr, a1_scr, p0_scr, p1_scr, *, tt):
    a_scr = (a0_scr, a1_scr)
    p_scr = (p0_scr, p1_scr)
    c = pl.program_id(1)
    last = pl.num_programs(1) - 1
    rows_per_chunk = PEER_CHUNK // PEER_NKEYS
    groups = PEER_NKEYS // SUBLANES

    @pl.when(c == 0)
    def _():
        acc_scr[...] = jnp.zeros(acc_scr.shape, F32)
        p_scr[1][...] = jnp.zeros(p_scr[1].shape, BF16)
        a_scr[0][...] = _dot(u0_ref[...], ht_ref[...])

    def key_rows(r, ls):
        i = c * rows_per_chunk + r
        d_rows = [jnp.broadcast_to(d_ref[pl.ds(hh * PEER_NKEYS + i, 1), ls], (SUBLANES, MXU_WIDTH))
                  for hh in range(PEER_HEADS)]
        e_rows = [jnp.broadcast_to(e1_ref[pl.ds(hh * PEER_NKEYS + i, 1), ls], (SUBLANES, MXU_WIDTH))
                  for hh in range(PEER_HEADS)]
        return d_rows, e_rows

    def weights_item(r, g2, ls, cur, rows):
        d_rows, e_rows = rows
        vals = []
        for g in (g2, g2 + 1):
            w = None
            for hh in range(PEER_HEADS):
                js = slice(hh * PEER_NKEYS + g * SUBLANES, hh * PEER_NKEYS + (g + 1) * SUBLANES)
                sel = jnp.where(s2_ref[js, ls] >= d_rows[hh], e_rows[hh] * e2_ref[js, ls], 0.0)
                w = sel if w is None else w + sel
            a = a_scr[cur][r * PEER_NKEYS + g * SUBLANES:r * PEER_NKEYS + (g + 1) * SUBLANES, ls]
            vals.append(_gelu(a) * w)
        lo = r * PEER_NKEYS + g2 * SUBLANES
        p_scr[cur][lo:lo + 2 * SUBLANES, ls] = jnp.concatenate(vals, axis=0).astype(BF16)

    def preact_piece(k, ls, nxt):
        ks = slice(k * MXU_WIDTH, (k + 1) * MXU_WIDTH)
        part = _dot(un_ref[:, ks], ht_ref[ks, ls])
        if k == 0:
            a_scr[nxt][:, ls] = part
        else:
            a_scr[nxt][:, ls] += part

    def value_piece(k, ls, nxt):
        ks = slice(k * MXU_WIDTH, (k + 1) * MXU_WIDTH)
        acc_scr[:, ls] += _dot(vtp_ref[:, ks], p_scr[nxt][ks, ls])

    def step(cur, nxt):
        n_pre = ht_ref.shape[0] // MXU_WIDTH
        n_val = PEER_CHUNK // MXU_WIDTH
        merged = sorted([((k + 0.5) / n_pre, 0, preact_piece, k) for k in range(n_pre)]
                        + [((k + 0.5) / n_val, 1, value_piece, k) for k in range(n_val)],
                        key=lambda e: e[:2])
        pieces = {r: [] for r in range(rows_per_chunk)}
        for idx, (_, _, fn, k) in enumerate(merged):
            pieces[idx * rows_per_chunk // len(merged)].append((fn, k))
        for t0 in range(0, tt, MXU_WIDTH):
            ls = slice(t0, t0 + MXU_WIDTH)
            for r in range(rows_per_chunk):
                for fn, k in pieces[r]:
                    fn(k, ls, nxt)
                rows = key_rows(r, ls)
                for g2 in range(0, groups, 2):
                    weights_item(r, g2, ls, cur, rows)

    @pl.when((c & 1) == 0)
    def _():
        step(0, 1)

    @pl.when((c & 1) == 1)
    def _():
        step(1, 0)

    @pl.when(c == last)
    def _():
        acc = acc_scr[...] + _dot(vtl_ref[...], p_scr[1][...])
        o_ref[...] = x_ref[...] + acc.T


def peer_dense(x, ht, d_thr, e1, s2, e2, u_tab, v_tab_t, *, tt=512):
    n, d = x.shape
    tt = min(tt, n)
    assert n % tt == 0 and tt % MXU_WIDTH == 0
    rows = PEER_HEADS * PEER_NKEYS
    nchunks = PEER_EXPERTS // PEER_CHUNK
    assert nchunks % 2 == 0
    col = lambda r: pl.BlockSpec((r, tt), lambda i, c: (0, i))
    return pl.pallas_call(
        functools.partial(_peer_dense_kernel, tt=tt), name="peer_dense",
        grid=(n // tt, nchunks),
        in_specs=[pl.BlockSpec((tt, d), lambda i, c: (i, 0)),
                  col(d), col(rows), col(rows), col(rows), col(rows),
                  pl.BlockSpec((PEER_CHUNK, d), lambda i, c: (0, 0)),
                  pl.BlockSpec((PEER_CHUNK, d), lambda i, c: (jnp.minimum(c + 1, nchunks - 1), 0)),
                  pl.BlockSpec((d, PEER_CHUNK), lambda i, c: (0, jnp.maximum(c - 1, 0))),
                  pl.BlockSpec((d, PEER_CHUNK), lambda i, c: (0, nchunks - 1))],
        out_specs=pl.BlockSpec((tt, d), lambda i, c: (i, 0)),
        out_shape=jax.ShapeDtypeStruct((n, d), F32),
        scratch_shapes=[pltpu.VMEM((d, tt), F32)] + [pltpu.VMEM((PEER_CHUNK, tt), F32)] * 2
        + [pltpu.VMEM((PEER_CHUNK, tt), BF16)] * 2,
        compiler_params=_params(("parallel", "arbitrary"), flags=PEER_DENSE_FLAGS))(
            x, ht, d_thr, e1, s2, e2, u_tab, u_tab, v_tab_t, v_tab_t)


def _norm_kernel(x_ref, g_ref, o_ref):
    o_ref[...] = _rms(x_ref[...], g_ref[...])


def final_norm(x, gain, *, tn=512):
    n, d = x.shape
    tn = min(tn, n)
    assert n % tn == 0
    tile = pl.BlockSpec((tn, d), lambda i: (i, 0))
    return pl.pallas_call(
        _norm_kernel, name="final_norm", grid=(n // tn,),
        in_specs=[tile, pl.BlockSpec((1, d), lambda i: (0, 0))], out_specs=tile,
        out_shape=jax.ShapeDtypeStruct((n, d), F32),
        compiler_params=_params(("parallel",)))(x, gain.reshape(1, d))


def _trunk(x, pos0, mem_k, mem_v, pool_past, conv_past, cache_k, cache_v, page_table, W):
    b, t, d = x.shape
    n = b * t
    x = x.reshape(n, d)
    new_k, new_v, new_pool, new_conv = [], [], [], []
    for l in range(DEPTH):
        i = l // 2
        if l % 2 == 0:
            lam_init = 0.8 - 0.6 * math.exp(-0.3 * l)
            q, k, v, p = fused_linear([x], [W["w_in_ab"][i]], name="mix_in_ab", gain=W["norm_mix"][l],
                                      splits=[A_QK] * 4)
            q3, k3, v3, p3 = (a.reshape(b, t, A_QK) for a in (q, k, v, p))
            lam_vecs = [W[nm][i] for nm in ("lambda_q1", "lambda_k1", "lambda_q2", "lambda_k2")]
            if page_table is None:
                o = dattn_prompt(q3, k3, v3, lam_vecs, W["subln_g"][i], lam_init)
            else:
                o = dattn_sample(q3, k3, v3, cache_k, cache_v, i, page_table, lam_vecs,
                                 W["subln_g"][i], lam_init)
            past = pool_past[i]
            past16 = jnp.pad(past, ((0, 0), (1, 0), (0, 0)))
            pooled = pool_mix(p3, past16, pos0, W["pool_w"][i], W["pool_scale"][i])
            x = fused_linear([o.reshape(n, A_QK), pooled.reshape(n, POOL_DIM)],
                             [W["w_out_ab"][i][:A_QK], W["w_out_ab"][i][A_QK:]], name="mix_out_ab", res=x)
            new_k.append(k3.reshape(b, t, A_HEADS, A_HD))
            new_v.append(v3.reshape(b, t, A_HEADS, A_HD))
            new_pool.append(jnp.concatenate([past, p3], axis=1)[:, -(POOL_MAX - 1):])
        else:
            bg, u = fused_linear([x], [W["w_in_c"][i]], name="mix_in_c", gain=W["norm_mix"][l], gate=True)
            u3 = u.reshape(b, t, d)
            past = conv_past[i]
            past8 = jnp.pad(past, ((0, 0), (CONV_HALO - (CONV_W - 1), 0), (0, 0)))
            x = short_conv(u3, bg.reshape(b, t, d), past8, W["conv_w"][i], W["w_out_c"][i],
                           x.reshape(b, t, d)).reshape(n, d)
            new_conv.append(jnp.concatenate([past, u3], axis=1)[:, -(CONV_W - 1):])
        if page_table is None:
            x = mem_block(x.reshape(b, t, d), W["norm_mem"][l], W["w_mq"][l], mem_k, mem_v, W["w_mo"][l],
                          l).reshape(n, d)
        else:
            qm = fused_linear([x], [W["w_mq"][l]], name="mem_q", gain=W["norm_mem"][l])
            om = mem_attn_rows(qm.reshape(b, t, d), mem_k, mem_v, l)
            x = fused_linear([om.reshape(n, d)], [W["w_mo"][l]], name="mem_o", res=x)
        sel = peer_select(x, W["norm_ffn"][l], W["peer_wq_t"][l], W["peer_subkeys"][l])
        x = peer_dense(x, *sel, W["peer_u"][l], W["peer_v_t"][l])
    y = final_norm(x, W["norm_final"]).reshape(b, t, d)
    return y, jnp.stack(new_k), jnp.stack(new_v), jnp.stack(new_pool), jnp.stack(new_conv)


def _bf(w):
    return w.astype(BF16)


def _prepare_weights(**w):
    out = dict(w)
    for name in ("w_in_ab", "w_out_ab", "pool_w", "w_in_c", "w_out_c", "w_mq", "w_mo", "peer_u"):
        out[name] = _bf(w[name])
    out["peer_wq_t"] = _bf(jnp.swapaxes(w["peer_wq"], 1, 2))
    out["peer_subkeys"] = _bf(w["peer_subkeys"].reshape(DEPTH, PEER_HEADS * 2, PEER_NKEYS, PEER_NKEYS))
    out["peer_v_t"] = _bf(jnp.swapaxes(w["peer_v"], 1, 2))
    return out


def kernel(x_prompt, x_sample, cache_attn_k, cache_attn_v, state_pool, state_conv, cache_mem_k, cache_mem_v, page_table, mem_prompt, norm_mix, norm_mem, norm_ffn, norm_final, w_in_ab, w_out_ab, lambda_q1, lambda_k1, lambda_q2, lambda_k2, subln_g, pool_w, pool_scale, w_in_c, conv_w, w_out_c, w_mq, w_mk, w_mv, w_mo, peer_wq, peer_subkeys, peer_u, peer_v):
    W = _prepare_weights(
        norm_mix=norm_mix, norm_mem=norm_mem, norm_ffn=norm_ffn, norm_final=norm_final, w_in_ab=w_in_ab,
        w_out_ab=w_out_ab, lambda_q1=lambda_q1, lambda_k1=lambda_k1, lambda_q2=lambda_q2, lambda_k2=lambda_k2,
        subln_g=subln_g, pool_w=pool_w, pool_scale=pool_scale, w_in_c=w_in_c, conv_w=conv_w, w_out_c=w_out_c,
        w_mq=w_mq, w_mo=w_mo, peer_wq=peer_wq, peer_subkeys=peer_subkeys, peer_u=peer_u, peer_v=peer_v)
    bp, sp, d = x_prompt.shape
    mk_p, mv_p = mem_project(mem_prompt.reshape(bp * N_MEM, d), _bf(w_mk), _bf(w_mv))
    mk_p = mk_p.reshape(DEPTH, bp, N_MEM, d)
    mv_p = mv_p.reshape(DEPTH, bp, N_MEM, d)
    pool0 = jnp.zeros((w_in_ab.shape[0], bp, POOL_MAX - 1, POOL_DIM), F32)
    conv0 = jnp.zeros((w_in_c.shape[0], bp, CONV_W - 1, d), F32)
    y_p, k_p, v_p, pool_p, conv_p = _trunk(x_prompt, 0, mk_p, mv_p, pool0, conv0, None, None, None, W)
    past_len = page_table.shape[1] * cache_attn_k.shape[2]
    n_layers, n_pool = cache_attn_k.shape[:2]
    ck = cache_attn_k.reshape(n_layers, n_pool, PAGE_SIZE * A_HEADS, A_HD)
    cv = cache_attn_v.reshape(n_layers, n_pool, PAGE_SIZE * A_HEADS, A_HD)
    y_s, k_s, v_s, pool_s, conv_s = _trunk(
        x_sample, past_len, _mem_rows_view(cache_mem_k), _mem_rows_view(cache_mem_v), state_pool, state_conv,
        ck, cv, page_table, W)
    mshape = (DEPTH, bp, N_MEM, MEM_HEADS, MEM_DH)
    return (y_p, y_s, k_p, v_p, pool_p, conv_p, mk_p.reshape(mshape), mv_p.reshape(mshape),
            k_s, v_s, pool_s, conv_s)
```
